```python
import jax, jax.numpy as jnp
from jax import lax
import numpy as np

D_MODEL = 1024
BATCH = 8
SEQ = 2048
DEPTH = 4
DEC_BATCH = 8
DEC_SEQ = 16
PAST_LEN = 1024

CHUNK = 64
N_HEADS = 8
HEAD_DIM = 64
ATT_W = N_HEADS * HEAD_DIM
IDX_HEADS = 4
IDX_DIM = 64
TOPK_MAX = 256
CONV_W = 512
CONV_K = 3
D_FF = 2816
Q_BLOCK = 64
ROPE_THETA = 10000.0
EPS = 1e-6
NEG = -1e30
IDX_SCALE = (IDX_DIM ** -0.5) * (IDX_HEADS ** -0.5)
IN_SIZES = (ATT_W, ATT_W, ATT_W, IDX_HEADS * IDX_DIM, IDX_DIM, IDX_HEADS, CONV_W, CONV_W, CONV_W, 2 * D_MODEL)
W_IN_COLS = sum(IN_SIZES)

kernel_name = 'dsa_shortconv_macaron_stream_step'


def rms_norm(x, g):
    xf = x.astype(jnp.float32)
    y = xf * lax.rsqrt(jnp.mean(xf * xf, axis=-1, keepdims=True) + EPS)
    return (y * g.astype(jnp.float32)).astype(x.dtype)


def half_swiglu(x, g, w_up, w_down):
    h = rms_norm(x, g)
    a, b = jnp.split(h @ w_up, 2, axis=-1)
    return x + 0.5 * ((jax.nn.silu(a) * b) @ w_down)


def rope(x, pos):
    half = x.shape[-1] // 2
    freqs = ROPE_THETA ** (-jnp.arange(half, dtype=jnp.float32) / half)
    ang = pos.astype(jnp.float32)[:, None] * freqs[None, :]
    cos = jnp.cos(ang)[:, None, :]
    sin = jnp.sin(ang)[:, None, :]
    xf = x.astype(jnp.float32)
    x1, x2 = xf[..., :half], xf[..., half:]
    return jnp.concatenate([x1 * cos - x2 * sin, x1 * sin + x2 * cos], axis=-1).astype(x.dtype)


def dsa_attend(q, qi, wi, k_all, v_all, ki_all, q_pos, k_pos, top_k):
    B = q.shape[0]
    s_idx = jnp.einsum('bthd,bsd->bths', qi, ki_all).astype(jnp.float32)
    score = jnp.einsum('bths,bth->bts', jax.nn.relu(s_idx), wi.astype(jnp.float32)) * IDX_SCALE
    q_chunk = q_pos // CHUNK
    k_chunk = k_pos // CHUNK
    adm = k_chunk[None, :] <= q_chunk[:, None]
    score = jnp.where(adm[None], score, NEG)
    _, idx = lax.top_k(score, top_k)
    valid = k_chunk[idx] <= q_chunk[None, :, None]
    bidx = jnp.arange(B)[:, None, None]
    k_sel = k_all[bidx, idx]
    v_sel = v_all[bidx, idx]
    logits = jnp.einsum('bthd,btkhd->bhtk', q, k_sel).astype(jnp.float32) * (HEAD_DIM ** -0.5)
    logits = jnp.where(valid[:, None], logits, NEG)
    p = jax.nn.softmax(logits, axis=-1).astype(v_sel.dtype)
    return jnp.einsum('bhtk,btkhd->bthd', p, v_sel)


def sparse_attention(q, qi, wi, k_all, v_all, ki_all, q_pos, k_pos, top_k):
    B, T = q.shape[0], q.shape[1]
    if T <= Q_BLOCK:
        return dsa_attend(q, qi, wi, k_all, v_all, ki_all, q_pos, k_pos, top_k)
    nb = T // Q_BLOCK

    def blk(a):
        return jnp.moveaxis(a.reshape((B, nb, Q_BLOCK) + a.shape[2:]), 1, 0)

    xs = (blk(q), blk(qi), blk(wi), q_pos.reshape(nb, Q_BLOCK))
    out = lax.map(lambda s: dsa_attend(s[0], s[1], s[2], k_all, v_all, ki_all, s[3], k_pos, top_k), xs)
    return jnp.moveaxis(out, 0, 1).reshape(B, T, N_HEADS, HEAD_DIM)


def split_cols(p):
    offs = []
    acc = 0
    for s in IN_SIZES[:-1]:
        acc += s
        offs.append(acc)
    return jnp.split(p, offs, axis=-1)


def layer(x, pos, k_past, v_past, ki_past, conv_past,
          ffn1_norm, ffn1_up, ffn1_down, mix_norm, w_in, conv_w, w_branch, w_out,
          ffn2_norm, ffn2_up, ffn2_down):
    B, T, _ = x.shape
    x = half_swiglu(x, ffn1_norm, ffn1_up, ffn1_down)
    h = rms_norm(x, mix_norm)
    q, k, v, qi, ki, wi, cb, cc, cx, gates = split_cols(h @ w_in)
    q = rope(q.reshape(B, T, N_HEADS, HEAD_DIM), pos)
    k = rope(k.reshape(B, T, N_HEADS, HEAD_DIM), pos)
    v = v.reshape(B, T, N_HEADS, HEAD_DIM)
    qi = rope(qi.reshape(B, T, IDX_HEADS, IDX_DIM), pos)
    ki = rope(ki[:, :, None, :], pos)[:, :, 0, :]
    u = cc * cx
    if k_past is None:
        k_all, v_all, ki_all = k, v, ki
        k_pos = pos
        conv_prev = jnp.zeros((B, CONV_K - 1, CONV_W), u.dtype)
    else:
        k_all = jnp.concatenate([k_past, k], axis=1)
        v_all = jnp.concatenate([v_past, v], axis=1)
        ki_all = jnp.concatenate([ki_past, ki], axis=1)
        k_pos = jnp.arange(k_all.shape[1])
        conv_prev = conv_past
    top_k = min(TOPK_MAX, k_all.shape[1] // 4)
    o_att = sparse_attention(q, qi, wi, k_all, v_all, ki_all, pos, k_pos, top_k).reshape(B, T, ATT_W)
    u_pad = jnp.concatenate([conv_prev, u], axis=1)
    y_conv = conv_w[0] * u_pad[:, 0:T]
    for j in range(1, CONV_K):
        y_conv = y_conv + conv_w[j] * u_pad[:, j:j + T]
    o_conv = cb * y_conv
    y_a = o_att @ w_branch[:ATT_W]
    y_b = o_conv @ w_branch[ATT_W:]
    g_a, g_b = jnp.split(jax.nn.sigmoid(gates), 2, axis=-1)
    x = x + (g_a * y_a + g_b * y_b) @ w_out
    x = half_swiglu(x, ffn2_norm, ffn2_up, ffn2_down)
    return x, k, v, ki, u_pad[:, -(CONV_K - 1):]


def setup_inputs(seed: int = 0) -> dict:
    key = jax.random.key(seed)
    ks = jax.random.split(key, 20)
    f = jnp.float32

    def w(k, shape, fan_in):
        return jax.random.normal(k, shape, f) * (fan_in ** -0.5)

    def gain(k, shape):
        return 1.0 + 0.01 * jax.random.normal(k, shape, f)

    return {
        'x_prompt': jax.random.normal(ks[0], (BATCH, SEQ, D_MODEL), f),
        'x_sample': jax.random.normal(ks[1], (DEC_BATCH, DEC_SEQ, D_MODEL), f),
        'cache_k': jax.random.normal(ks[2], (DEPTH, DEC_BATCH, PAST_LEN, N_HEADS, HEAD_DIM), f),
        'cache_v': jax.random.normal(ks[3], (DEPTH, DEC_BATCH, PAST_LEN, N_HEADS, HEAD_DIM), f),
        'cache_kidx': jax.random.normal(ks[4], (DEPTH, DEC_BATCH, PAST_LEN, IDX_DIM), f),
        'state_conv': jax.random.normal(ks[5], (DEPTH, DEC_BATCH, CONV_K - 1, CONV_W), f),
        'ffn1_norm': gain(ks[6], (DEPTH, D_MODEL)),
        'ffn1_up': w(ks[7], (DEPTH, D_MODEL, 2 * D_FF), D_MODEL),
        'ffn1_down': w(ks[8], (DEPTH, D_FF, D_MODEL), D_FF),
        'mix_norm': gain(ks[9], (DEPTH, D_MODEL)),
        'w_in': w(ks[10], (DEPTH, D_MODEL, W_IN_COLS), D_MODEL),
        'conv_w': w(ks[11], (DEPTH, CONV_K, CONV_W), CONV_K),
        'w_branch': w(ks[12], (DEPTH, ATT_W + CONV_W, D_MODEL), ATT_W),
        'w_out': w(ks[13], (DEPTH, D_MODEL, D_MODEL), D_MODEL),
        'ffn2_norm': gain(ks[14], (DEPTH, D_MODEL)),
        'ffn2_up': w(ks[15], (DEPTH, D_MODEL, 2 * D_FF), D_MODEL),
        'ffn2_down': w(ks[16], (DEPTH, D_FF, D_MODEL), D_FF),
        'final_norm': gain(ks[17], (D_MODEL,)),
    }


def reference(x_prompt, x_sample, cache_k, cache_v, cache_kidx, state_conv,
              ffn1_norm, ffn1_up, ffn1_down, mix_norm, w_in, conv_w, w_branch, w_out,
              ffn2_norm, ffn2_up, ffn2_down, final_norm):
    pos_p = jnp.arange(x_prompt.shape[1])
    pos_s = cache_k.shape[2] + jnp.arange(x_sample.shape[1])
    xp, xs = x_prompt, x_sample
    kp, vp, kip, cp = [], [], [], []
    kss, vss, kis, cs = [], [], [], []
    for i in range(DEPTH):
        wts = (ffn1_norm[i], ffn1_up[i], ffn1_down[i], mix_norm[i], w_in[i], conv_w[i],
               w_branch[i], w_out[i], ffn2_norm[i], ffn2_up[i], ffn2_down[i])
        xp, k1, v1, ki1, c1 = layer(xp, pos_p, None, None, None, None, *wts)
        xs, k2, v2, ki2, c2 = layer(xs, pos_s, cache_k[i], cache_v[i], cache_kidx[i], state_conv[i], *wts)
        kp.append(k1); vp.append(v1); kip.append(ki1); cp.append(c1)
        kss.append(k2); vss.append(v2); kis.append(ki2); cs.append(c2)
    y_prompt = rms_norm(xp, final_norm)
    y_sample = rms_norm(xs, final_norm)
    return (y_prompt, y_sample,
            jnp.stack(kp), jnp.stack(vp), jnp.stack(kip), jnp.stack(cp),
            jnp.stack(kss), jnp.stack(vss), jnp.stack(kis), jnp.stack(cs))
```

```python
import functools

import jax
import jax.numpy as jnp
from jax import lax
from jax.experimental import pallas as pl
from jax.experimental.pallas import tpu as pltpu

CHUNK = 64
N_HEADS = 8
HEAD_DIM = 64
ATT_W = N_HEADS * HEAD_DIM
IDX_HEADS = 4
IDX_DIM = 64
TOPK_MAX = 256
CONV_W = 512
CONV_K = 3
ROPE_THETA = 10000.0
EPS = 1e-6
NEG = -1e30
BIG = 3e38
IDX_SCALE = (IDX_DIM ** -0.5) * (IDX_HEADS ** -0.5)

LANES = 128
SUBLANES = 8
VMEM_LIMIT = 56 * 1024 * 1024
N_BISECT = 20

BF16 = jnp.bfloat16
F32 = jnp.float32


def _dot(a, b):
    return jnp.dot(a, b, preferred_element_type=F32)


def _rms(x, g):
    return x * lax.rsqrt(jnp.mean(x * x, axis=-1, keepdims=True) + EPS) * g


def _const_spec(shape):
    nd = len(shape)
    return pl.BlockSpec(shape, lambda *_: (0,) * nd, pipeline_mode=pl.Buffered(1))


def _ffn_kernel(*refs, d_ff, n_chunk, has_mix, has_final):
    it = iter(refs)
    x_ref = next(it)
    if has_mix:
        oatt_ref, gbyb_ref, mixg_ref, wga_ref, wb1_ref, wout_ref = (next(it) for _ in range(6))
    g_ref, wup_ref, wdn_ref = next(it), next(it), next(it)
    fin_ref = next(it) if has_final else None
    o_ref = next(it)

    x = x_ref[...]
    if has_mix:
        hm = _rms(x, mixg_ref[...]).astype(BF16)
        g_a = jax.nn.sigmoid(_dot(hm, wga_ref[...]))
        y_a = _dot(oatt_ref[...], wb1_ref[...])
        mixed = g_a * y_a + gbyb_ref[...].astype(F32)
        x = x + _dot(mixed.astype(BF16), wout_ref[...])
    h = _rms(x, g_ref[...]).astype(BF16)
    cw = d_ff // n_chunk
    acc = jnp.zeros(x.shape, F32)
    for c in range(n_chunk):
        a = _dot(h, wup_ref[:, c * cw:(c + 1) * cw])
        b = _dot(h, wup_ref[:, d_ff + c * cw:d_ff + (c + 1) * cw])
        act = (a * jax.nn.sigmoid(a) * b).astype(BF16)
        acc = acc + _dot(act, wdn_ref[c * cw:(c + 1) * cw, :])
    y = x + 0.5 * acc
    if has_final:
        y = _rms(y, fin_ref[...])
    o_ref[...] = y


def _ffn_call(x, g, w_up, w_dn, *, mix=None, final_g=None, tm):
    n, d = x.shape
    d_ff = w_dn.shape[0]
    n_chunk = 2 if d_ff % (2 * LANES) == 0 else 1
    row = lambda w: pl.BlockSpec((tm, w), lambda i: (i, 0))
    args, specs = [x], [row(d)]
    if mix is not None:
        oatt, gbyb, mix_g, wga, wb1, wout = mix
        args += [oatt, gbyb, mix_g, wga, wb1, wout]
        specs += [row(oatt.shape[1]), row(d), _const_spec(mix_g.shape), _const_spec(wga.shape),
                  _const_spec(wb1.shape), _const_spec(wout.shape)]
    args += [g, w_up, w_dn]
    specs += [_const_spec(g.shape), _const_spec(w_up.shape), _const_spec(w_dn.shape)]
    if final_g is not None:
        args.append(final_g)
        specs.append(_const_spec(final_g.shape))
    kern = functools.partial(_ffn_kernel, d_ff=d_ff, n_chunk=n_chunk,
                             has_mix=mix is not None, has_final=final_g is not None)
    return pl.pallas_call(
        kern,
        out_shape=jax.ShapeDtypeStruct((n, d), F32),
        grid=(n // tm,),
        in_specs=specs,
        out_specs=row(d),
        compiler_params=pltpu.CompilerParams(dimension_semantics=("arbitrary",),
                                             vmem_limit_bytes=VMEM_LIMIT),
        name="ffn_mix" if mix is not None else "ffn",
    )(*args)


def _rope(y, c, s):
    n = y.shape[-1]
    lane = lax.broadcasted_iota(jnp.int32, y.shape, 1)
    first = (lane & (HEAD_DIM - 1)) < HEAD_DIM // 2
    rot = jnp.where(first, pltpu.roll(y, n - HEAD_DIM // 2, 1), pltpu.roll(y, HEAD_DIM // 2, 1))
    return y * c + rot * s


def _proj_kernel(x_ref, g_ref, wqkv_ref, wqi_ref, wkw_ref, wconv_ref, wgb_ref, wb2_ref,
                 convw_ref, cprev_ref, cos_ref, sin_ref, coskw_ref, sinkw_ref,
                 kf_ref, vf_ref, kif_ref, kb_ref, kib_ref, q_ref, v_ref, qi_ref, wi_ref,
                 gbyb_ref, nconv_ref, ubuf, *, tm, transposed):
    t = pl.program_id(1)
    h = _rms(x_ref[0], g_ref[...]).astype(BF16)
    cosf, sinf = cos_ref[...], sin_ref[...]
    idx_w = IDX_HEADS * IDX_DIM

    q = _rope(_dot(h, wqkv_ref[:, 0:ATT_W]), cosf, sinf) * (HEAD_DIM ** -0.5)
    k = _rope(_dot(h, wqkv_ref[:, ATT_W:2 * ATT_W]), cosf, sinf)
    v = _dot(h, wqkv_ref[:, 2 * ATT_W:3 * ATT_W])
    qi = _rope(_dot(h, wqi_ref[...]), cosf[:, :idx_w], sinf[:, :idx_w])
    kw = _rope(_dot(h, wkw_ref[...]), coskw_ref[...], sinkw_ref[...])

    kf_ref[0] = k
    vf_ref[0] = v
    kif_ref[0] = kw[:, :IDX_DIM]
    kb_ref[0, 0] = k.astype(BF16)
    kib_ref[0, 0] = kw[:, :IDX_DIM].astype(BF16)
    if transposed:
        q_ref[0] = q.T.astype(BF16)
        v_ref[0, 0] = v.T.astype(BF16)
        qi_ref[0] = qi.T.astype(BF16)
        wi_ref[0] = kw.T[IDX_DIM:IDX_DIM + SUBLANES, :]
    else:
        q_ref[0] = q.astype(BF16)
        v_ref[0] = v.astype(BF16)
        qi_ref[0] = qi.astype(BF16)
        wi_ref[0] = kw

    cb = _dot(h, wconv_ref[:, 0:CONV_W])
    cc = _dot(h, wconv_ref[:, CONV_W:2 * CONV_W])
    cx = _dot(h, wconv_ref[:, 2 * CONV_W:3 * CONV_W])
    u = cc * cx

    @pl.when(t == 0)
    def _():
        ubuf[SUBLANES - 2:SUBLANES, :] = cprev_ref[0]

    ubuf[SUBLANES:SUBLANES + tm, :] = u
    um1 = ubuf[SUBLANES - 1:SUBLANES - 1 + tm, :]
    um2 = ubuf[SUBLANES - 2:SUBLANES - 2 + tm, :]
    cw = convw_ref[...]
    y_conv = cw[0:1, :] * um2 + cw[1:2, :] * um1 + cw[2:3, :] * u
    tail = u[tm - 2:tm, :]
    ubuf[SUBLANES - 2:SUBLANES, :] = tail
    nconv_ref[0] = tail

    y_b = _dot((cb * y_conv).astype(BF16), wb2_ref[...])
    g_b = jax.nn.sigmoid(_dot(h, wgb_ref[...]))
    gbyb_ref[0] = (g_b * y_b).astype(BF16)


def _proj_call(x, w, conv_prev, tabs, *, tm, transposed):
    b, t, d = x.shape
    nt = t // tm
    cos, sin, coskw, sinkw = tabs
    idx_w = IDX_HEADS * IDX_DIM
    consts = [w["mix_g"], w["wqkv"], w["wqi"], w["wkw"], w["wconv"], w["wgb"], w["wb2"], w["conv_w"]]
    tile3 = lambda width: pl.BlockSpec((1, tm, width), lambda i, j: (i, j, 0))
    tile4 = lambda width: pl.BlockSpec((1, 1, tm, width), lambda i, j: (i, j, 0, 0))
    tab = lambda a: pl.BlockSpec((tm, a.shape[1]), lambda i, j: (j, 0))
    in_specs = ([tile3(d)] + [_const_spec(c.shape) for c in consts]
                + [pl.BlockSpec((1, CONV_K - 1, CONV_W), lambda i, j: (i, 0, 0))]
                + [tab(cos), tab(sin), tab(coskw), tab(sinkw)])
    out_shape = [
        jax.ShapeDtypeStruct((b, t, ATT_W), F32),
        jax.ShapeDtypeStruct((b, t, ATT_W), F32),
        jax.ShapeDtypeStruct((b, t, IDX_DIM), F32),
        jax.ShapeDtypeStruct((b, nt, tm, ATT_W), BF16),
        jax.ShapeDtypeStruct((b, nt, tm, IDX_DIM), BF16),
    ]
    out_specs = [tile3(ATT_W), tile3(ATT_W), tile3(IDX_DIM), tile4(ATT_W), tile4(IDX_DIM)]
    if transposed:
        out_shape += [
            jax.ShapeDtypeStruct((b, ATT_W, t), BF16),
            jax.ShapeDtypeStruct((b, nt, ATT_W, tm), BF16),
            jax.ShapeDtypeStruct((b, idx_w, t), BF16),
            jax.ShapeDtypeStruct((b, SUBLANES, t), F32),
        ]
        out_specs += [
            pl.BlockSpec((1, ATT_W, tm), lambda i, j: (i, 0, j)),
            pl.BlockSpec((1, 1, ATT_W, tm), lambda i, j: (i, j, 0, 0)),
            pl.BlockSpec((1, idx_w, tm), lambda i, j: (i, 0, j)),
            pl.BlockSpec((1, SUBLANES, tm), lambda i, j: (i, 0, j)),
        ]
    else:
        out_shape += [
            jax.ShapeDtypeStruct((b, t, ATT_W), BF16),
            jax.ShapeDtypeStruct((b, t, ATT_W), BF16),
            jax.ShapeDtypeStruct((b, t, idx_w), BF16),
            jax.ShapeDtypeStruct((b, t, LANES), F32),
        ]
        out_specs += [tile3(ATT_W), tile3(ATT_W), tile3(idx_w), tile3(LANES)]
    out_shape += [
        jax.ShapeDtypeStruct((b, t, d), BF16),
        jax.ShapeDtypeStruct((b, CONV_K - 1, CONV_W), F32),
    ]
    out_specs += [tile3(d), pl.BlockSpec((1, CONV_K - 1, CONV_W), lambda i, j: (i, 0, 0))]
    kern = functools.partial(_proj_kernel, tm=tm, transposed=transposed)
    return pl.pallas_call(
        kern,
        out_shape=out_shape,
        grid=(b, nt),
        in_specs=in_specs,
        out_specs=out_specs,
        scratch_shapes=[pltpu.VMEM((tm + SUBLANES, CONV_W), F32)],
        compiler_params=pltpu.CompilerParams(dimension_semantics=("arbitrary", "arbitrary"),
                                             vmem_limit_bytes=VMEM_LIMIT),
        name="proj",
    )(x, *consts, conv_prev, cos, sin, coskw, sinkw)


def _slab_reduce(x, op):
    kb, qw = x.shape
    return op(x.reshape(kb // SUBLANES, SUBLANES, qw), axis=0)


def _attend_kernel(qi_ref, wi_ref, ki_ref, k_ref, q_ref, v_ref, o_ref, s_ref, lg_ref, ot_ref,
                   *, kb_rows, qw, n_kb_static, q_pos0, n_valid, top_k):
    j = pl.program_id(1)
    n_kb = (j + 1) if n_kb_static is None else n_kb_static
    kf = float(top_k)

    def over_blocks(body, init):
        return lax.fori_loop(0, n_kb, body, init)

    last = n_kb - 1
    row = lax.broadcasted_iota(jnp.int32, (kb_rows, qw), 0) + last * kb_rows
    qpos = lax.broadcasted_iota(jnp.int32, (kb_rows, qw), 1) + (q_pos0 + j * qw)
    chunk_shift = CHUNK.bit_length() - 1
    adm_last = ((row >> chunk_shift) <= (qpos >> chunk_shift)) & (row < n_valid)

    def score_block(kb):
        ki = ki_ref[0, kb]
        s = jnp.zeros((kb_rows, qw), F32)
        for hh in range(IDX_HEADS):
            d = _dot(ki, qi_ref[0, hh * IDX_DIM:(hh + 1) * IDX_DIM, :])
            s = s + jnp.maximum(d, 0.0) * wi_ref[0, hh:hh + 1, :]
        return s * IDX_SCALE

    def score_body(kb, carry):
        mx, mn = carry
        s = score_block(kb)
        s_ref[kb] = s
        return (jnp.maximum(mx, _slab_reduce(s, jnp.max)), jnp.minimum(mn, _slab_reduce(s, jnp.min)))

    mx8 = jnp.full((SUBLANES, qw), -BIG, F32)
    mn8 = jnp.full((SUBLANES, qw), BIG, F32)
    mx8, mn8 = lax.fori_loop(0, last, score_body, (mx8, mn8))
    s_last = score_block(last)
    s_ref[last] = jnp.where(adm_last, s_last, NEG)
    mx8 = jnp.maximum(mx8, _slab_reduce(jnp.where(adm_last, s_last, -BIG), jnp.max))
    mn8 = jnp.minimum(mn8, _slab_reduce(jnp.where(adm_last, s_last, BIG), jnp.min))
    smax = jnp.max(mx8, axis=0, keepdims=True)
    smin = jnp.min(mn8, axis=0, keepdims=True)
    n_adm = (jnp.sum(_slab_reduce(jnp.where(adm_last, 1.0, 0.0), jnp.sum), axis=0, keepdims=True)
             + jnp.float32(1.0) * (last * kb_rows))

    def count(pred_fn):
        def body(kb, acc):
            return acc + _slab_reduce(jnp.where(pred_fn(s_ref[kb]), 1.0, 0.0), jnp.sum)
        return jnp.sum(over_blocks(body, jnp.zeros((SUBLANES, qw), F32)), axis=0, keepdims=True)

    def bisect(_, carry):
        lo, hi = carry
        mid = 0.5 * lo + 0.5 * hi
        ge = count(lambda s: s >= mid) >= kf
        return jnp.where(ge, mid, lo), jnp.where(ge, hi, mid)

    hi0 = smax + (jnp.abs(smax) + 1.0) * (2.0 ** -10)
    _, hi = lax.fori_loop(0, N_BISECT, bisect, (smin, hi0))

    def snap(carry):
        hi, done, tau, n_gt = carry

        def vbody(kb, acc):
            s = s_ref[kb]
            return jnp.maximum(acc, _slab_reduce(jnp.where(s < hi, s, -BIG), jnp.max))
        cand = jnp.max(over_blocks(vbody, jnp.full((SUBLANES, qw), -BIG, F32)), axis=0, keepdims=True)
        c_ge = count(lambda s: s >= cand)
        c_gt = count(lambda s: s > cand)
        ok = c_ge >= kf
        fresh = ok & (done < 0.5)
        return (jnp.where(ok, hi, cand), jnp.where(ok, 1.0, done),
                jnp.where(fresh, cand, tau), jnp.where(fresh, c_gt, n_gt))

    few = n_adm <= kf
    carry0 = (hi, jnp.where(few, 1.0, 0.0), jnp.full((1, qw), 0.5 * NEG, F32), jnp.zeros((1, qw), F32))
    _, _, tau, n_gt = lax.while_loop(lambda c: jnp.min(c[1]) < 0.5, snap, carry0)
    need = kf - n_gt

    r_i = lax.broadcasted_iota(jnp.int32, (kb_rows, kb_rows), 0)
    c_i = lax.broadcasted_iota(jnp.int32, (kb_rows, kb_rows), 1)
    tri = jnp.where(c_i <= r_i, 1.0, 0.0).astype(BF16)

    def select_body(kb, off):
        s = s_ref[kb]
        eq = s == tau
        rank = _dot(tri, jnp.where(eq, 1.0, 0.0).astype(BF16))
        y = jnp.where(eq, rank + off, jnp.where(s > tau, 0.0, BIG))
        s_ref[kb] = jnp.where(y <= need, 0.0, NEG)
        return off + rank[kb_rows - 1:kb_rows, :]

    over_blocks(select_body, jnp.zeros((1, qw), F32))

    pair_w = 2 * HEAD_DIM
    pr = lax.broadcasted_iota(jnp.int32, (pair_w, qw), 0)
    for hd in range(N_HEADS):
        pair, half = hd // 2, hd % 2
        qp = q_ref[0, pair * pair_w:(pair + 1) * pair_w, :]
        in_head = (pr >= half * HEAD_DIM) & (pr < (half + 1) * HEAD_DIM)
        q_pad = jnp.where(in_head, qp, jnp.zeros_like(qp))

        def logit_body(kb, m8):
            lg = _dot(k_ref[0, kb, :, pair * pair_w:(pair + 1) * pair_w], q_pad) + s_ref[kb]
            lg_ref[kb] = lg
            return jnp.maximum(m8, _slab_reduce(lg, jnp.max))

        m = jnp.max(over_blocks(logit_body, jnp.full((SUBLANES, qw), -BIG, F32)), axis=0, keepdims=True)

        def pv_body(kb, carry):
            l8, acc = carry
            e = jnp.exp(lg_ref[kb] - m)
            vt = v_ref[0, kb, hd * HEAD_DIM:(hd + 1) * HEAD_DIM, :]
            return l8 + _slab_reduce(e, jnp.sum), acc + _dot(vt, e.astype(BF16))

        l8, acc = over_blocks(pv_body, (jnp.zeros((SUBLANES, qw), F32), jnp.zeros((HEAD_DIM, qw), F32)))
        ot_ref[hd * HEAD_DIM:(hd + 1) * HEAD_DIM, :] = acc / jnp.sum(l8, axis=0, keepdims=True)

    o_ref[0] = ot_ref[...].T.astype(BF16)


def _attend_call(qi_t, wi_t, ki_b, k_b, q_t, v_tb, *, qw, n_kb_static, q_pos0, n_valid, top_k):
    b, n_kb_max, kb_rows, _ = k_b.shape
    t_q = q_t.shape[2]
    n_q = t_q // qw
    idx_w = IDX_HEADS * IDX_DIM
    batch_blk = lambda shape: pl.BlockSpec((1,) + shape, lambda i, j: (i,) + (0,) * len(shape))
    kern = functools.partial(_attend_kernel, kb_rows=kb_rows, qw=qw, n_kb_static=n_kb_static,
                             q_pos0=q_pos0, n_valid=n_valid, top_k=top_k)
    return pl.pallas_call(
        kern,
        out_shape=jax.ShapeDtypeStruct((b, t_q, ATT_W), BF16),
        grid=(b, n_q),
        in_specs=[
            pl.BlockSpec((1, idx_w, qw), lambda i, j: (i, 0, j)),
            pl.BlockSpec((1, SUBLANES, qw), lambda i, j: (i, 0, j)),
            batch_blk((n_kb_max, kb_rows, IDX_DIM)),
            batch_blk((n_kb_max, kb_rows, ATT_W)),
            pl.BlockSpec((1, ATT_W, qw), lambda i, j: (i, 0, j)),
            batch_blk((n_kb_max, ATT_W, kb_rows)),
        ],
        out_specs=pl.BlockSpec((1, qw, ATT_W), lambda i, j: (i, j, 0)),
        scratch_shapes=[pltpu.VMEM((n_kb_max, kb_rows, qw), F32),
                        pltpu.VMEM((n_kb_max, kb_rows, qw), F32),
                        pltpu.VMEM((ATT_W, qw), F32)],
        compiler_params=pltpu.CompilerParams(dimension_semantics=("arbitrary", "arbitrary"),
                                             vmem_limit_bytes=VMEM_LIMIT),
        name="attend",
    )(qi_t, wi_t, ki_b, k_b, q_t, v_tb)


def _rope_tables(pos):
    half = HEAD_DIM // 2
    freqs = ROPE_THETA ** (-jnp.arange(half, dtype=F32) / half)
    ang = pos.astype(F32)[:, None] * freqs[None, :]
    cos, sin = jnp.cos(ang), jnp.sin(ang)
    cos_h = jnp.concatenate([cos, cos], axis=-1)
    sin_h = jnp.concatenate([-sin, sin], axis=-1)
    n = pos.shape[0]
    pad_c = jnp.ones((n, LANES - IDX_DIM), F32)
    pad_s = jnp.zeros((n, LANES - IDX_DIM), F32)
    return (jnp.tile(cos_h, (1, N_HEADS)), jnp.tile(sin_h, (1, N_HEADS)),
            jnp.concatenate([cos_h, pad_c], axis=-1), jnp.concatenate([sin_h, pad_s], axis=-1))


def _layer_weights(i, ffn1_norm, ffn1_up, ffn1_down, mix_norm, w_in, conv_w, w_branch, w_out,
                   ffn2_norm, ffn2_up, ffn2_down):
    d = w_in.shape[1]
    idx_w = IDX_HEADS * IDX_DIM
    o = 0
    wi = w_in[i]
    wqkv = wi[:, o:o + 3 * ATT_W]; o += 3 * ATT_W
    wqi = wi[:, o:o + idx_w]; o += idx_w
    wkw = wi[:, o:o + IDX_DIM + IDX_HEADS]; o += IDX_DIM + IDX_HEADS
    wconv = wi[:, o:o + 3 * CONV_W]; o += 3 * CONV_W
    wga = wi[:, o:o + d]; o += d
    wgb = wi[:, o:o + d]
    wkw = jnp.pad(wkw, ((0, 0), (0, LANES - IDX_DIM - IDX_HEADS)))
    c = lambda a: a.astype(BF16)
    return dict(
        ffn1_g=ffn1_norm[i][None], ffn1_up=c(ffn1_up[i]), ffn1_dn=c(ffn1_down[i]),
        mix_g=mix_norm[i][None], wqkv=c(wqkv), wqi=c(wqi), wkw=c(wkw), wconv=c(wconv),
        wga=c(wga), wgb=c(wgb), conv_w=conv_w[i],
        wb1=c(w_branch[i][:ATT_W]), wb2=c(w_branch[i][ATT_W:]), wout=c(w_out[i]),
        ffn2_g=ffn2_norm[i][None], ffn2_up=c(ffn2_up[i]), ffn2_dn=c(ffn2_down[i]),
    )


def _pick_tile(n, pref):
    t = min(pref, n)
    while n % t:
        t //= 2
    return t


def kernel(x_prompt, x_sample, cache_k, cache_v, cache_kidx, state_conv, ffn1_norm, ffn1_up, ffn1_down, mix_norm, w_in, conv_w, w_branch, w_out, ffn2_norm, ffn2_up, ffn2_down, final_norm):
    depth = w_in.shape[0]
    bp, tp, d = x_prompt.shape
    bs, ts, _ = x_sample.shape
    past = cache_k.shape[2]
    qw_p = 2 * LANES
    kb_s = LANES
    l_s = past + ts
    n_kb_s = -(-l_s // kb_s)
    pad_s = n_kb_s * kb_s - l_s
    topk_p = min(TOPK_MAX, tp // 4)
    topk_s = min(TOPK_MAX, l_s // 4)

    tabs_p = _rope_tables(jnp.arange(tp))
    tabs_s = _rope_tables(past + jnp.arange(ts))
    fin = final_norm[None]
    zero_conv = jnp.zeros((bp, CONV_K - 1, CONV_W), F32)
    tm_p = _pick_tile(bp * tp, 512)
    tm_s = _pick_tile(bs * ts, 512)

    xp = x_prompt.reshape(bp * tp, d)
    xs = x_sample.reshape(bs * ts, d)
    outs = [[] for _ in range(8)]
    for i in range(depth):
        w = _layer_weights(i, ffn1_norm, ffn1_up, ffn1_down, mix_norm, w_in, conv_w, w_branch, w_out,
                           ffn2_norm, ffn2_up, ffn2_down)
        last = i == depth - 1
        mixw = lambda oatt, gbyb: (oatt, gbyb, w["mix_g"], w["wga"], w["wb1"], w["wout"])

        xp = _ffn_call(xp, w["ffn1_g"], w["ffn1_up"], w["ffn1_dn"], tm=tm_p)
        (kf, vf, kif, kb, kib, qt, vtb, qit, wit, gbyb, nconv) = _proj_call(
            xp.reshape(bp, tp, d), w, zero_conv, tabs_p, tm=qw_p, transposed=True)
        oatt = _attend_call(qit, wit, kib, kb, qt, vtb, qw=qw_p, n_kb_static=None,
                            q_pos0=0, n_valid=tp, top_k=topk_p)
        xp = _ffn_call(xp, w["ffn2_g"], w["ffn2_up"], w["ffn2_dn"],
                       mix=mixw(oatt.reshape(bp * tp, ATT_W), gbyb.reshape(bp * tp, d)),
                       final_g=fin if last else None, tm=tm_p)
        for lst, a in zip(outs[:4], (kf, vf, kif, nconv)):
            lst.append(a)

        xs = _ffn_call(xs, w["ffn1_g"], w["ffn1_up"], w["ffn1_dn"], tm=tm_s)
        (kf, vf, kif, kb, kib, qn, vn, qin, win, gbyb, nconv) = _proj_call(
            xs.reshape(bs, ts, d), w, state_conv[i], tabs_s, tm=ts, transposed=False)
        lane_pad = lambda a: jnp.pad(jnp.swapaxes(a, 1, 2), ((0, 0), (0, 0), (0, kb_s - ts)))
        key_blocks = lambda past_rows, new_rows: jnp.pad(
            jnp.concatenate([past_rows.astype(BF16), new_rows], axis=1),
            ((0, 0), (0, pad_s), (0, 0))).reshape(bs, n_kb_s, kb_s, -1)
        k_all = key_blocks(cache_k[i].reshape(bs, past, ATT_W), kb[:, 0])
        ki_all = key_blocks(cache_kidx[i], kib[:, 0])
        v_all = jnp.swapaxes(key_blocks(cache_v[i].reshape(bs, past, ATT_W), vn), 2, 3)
        oatt = _attend_call(lane_pad(qin), lane_pad(win[:, :, IDX_DIM:IDX_DIM + SUBLANES]), ki_all, k_all,
                            lane_pad(qn), v_all, qw=kb_s, n_kb_static=n_kb_s,
                            q_pos0=past, n_valid=l_s, top_k=topk_s)[:, :ts]
        xs = _ffn_call(xs, w["ffn2_g"], w["ffn2_up"], w["ffn2_dn"],
                       mix=mixw(oatt.reshape(bs * ts, ATT_W), gbyb.reshape(bs * ts, d)),
                       final_g=fin if last else None, tm=tm_s)
        for lst, a in zip(outs[4:], (kf, vf, kif, nconv)):
            lst.append(a)

    heads = lambda a: a.reshape(a.shape[:-1] + (N_HEADS, HEAD_DIM))
    st = [jnp.stack(l) for l in outs]
    return (xp.reshape(bp, tp, d), xs.reshape(bs, ts, d),
            heads(st[0]), heads(st[1]), st[2], st[3],
            heads(st[4]), heads(st[5]), st[6], st[7])
```

```python
import functools

import jax
import jax.numpy as jnp
from jax import lax
from jax.experimental import pallas as pl
from jax.experimental.pallas import tpu as pltpu

CHUNK = 64
N_HEADS = 8
HEAD_DIM = 64
ATT_W = N_HEADS * HEAD_DIM
IDX_HEADS = 4
IDX_DIM = 64
TOPK_MAX = 256
CONV_W = 512
CONV_K = 3
ROPE_THETA = 10000.0
EPS = 1e-6
NEG = -1e30
BIG = 3e38
IDX_SCALE = (IDX_DIM ** -0.5) * (IDX_HEADS ** -0.5)

LANES = 128
SUBLANES = 8
VMEM_LIMIT = 56 * 1024 * 1024
N_BISECT = 20

BF16 = jnp.bfloat16
F32 = jnp.float32


def _dot(a, b):
    return jnp.dot(a, b, preferred_element_type=F32)


def _rms(x, g):
    return x * lax.rsqrt(jnp.mean(x * x, axis=-1, keepdims=True) + EPS) * g


def _const_spec(shape):
    nd = len(shape)
    return pl.BlockSpec(shape, lambda *_: (0,) * nd, pipeline_mode=pl.Buffered(1))


def _ffn_kernel(*refs, d_ff, n_chunk, has_mix, has_final):
    it = iter(refs)
    x_ref = next(it)
    if has_mix:
        oatt_ref, gbyb_ref, mixg_ref, wga_ref, wb1_ref, wout_ref = (next(it) for _ in range(6))
    g_ref, wup_ref, wdn_ref = next(it), next(it), next(it)
    fin_ref = next(it) if has_final else None
    o_ref = next(it)

    x = x_ref[...]
    if has_mix:
        hm = _rms(x, mixg_ref[...]).astype(BF16)
        g_a = jax.nn.sigmoid(_dot(hm, wga_ref[...]))
        y_a = _dot(oatt_ref[...], wb1_ref[...])
        mixed = g_a * y_a + gbyb_ref[...].astype(F32)
        x = x + _dot(mixed.astype(BF16), wout_ref[...])
    h = _rms(x, g_ref[...]).astype(BF16)
    cw = d_ff // n_chunk
    acc = jnp.zeros(x.shape, F32)
    for c in range(n_chunk):
        a = _dot(h, wup_ref[:, c * cw:(c + 1) * cw])
        b = _dot(h, wup_ref[:, d_ff + c * cw:d_ff + (c + 1) * cw])
        act = (a * jax.nn.sigmoid(a) * b).astype(BF16)
        acc = acc + _dot(act, wdn_ref[c * cw:(c + 1) * cw, :])
    y = x + 0.5 * acc
    if has_final:
        y = _rms(y, fin_ref[...])
    o_ref[...] = y


def _ffn_call(x, g, w_up, w_dn, *, mix=None, final_g=None, tm):
    n, d = x.shape
    d_ff = w_dn.shape[0]
    n_chunk = 2 if d_ff % (2 * LANES) == 0 else 1
    row = lambda w: pl.BlockSpec((tm, w), lambda i: (i, 0))
    args, specs = [x], [row(d)]
    if mix is not None:
        oatt, gbyb, mix_g, wga, wb1, wout = mix
        args += [oatt, gbyb, mix_g, wga, wb1, wout]
        specs += [row(oatt.shape[1]), row(d), _const_spec(mix_g.shape), _const_spec(wga.shape),
                  _const_spec(wb1.shape), _const_spec(wout.shape)]
    args += [g, w_up, w_dn]
    specs += [_const_spec(g.shape), _const_spec(w_up.shape), _const_spec(w_dn.shape)]
    if final_g is not None:
        args.append(final_g)
        specs.append(_const_spec(final_g.shape))
    kern = functools.partial(_ffn_kernel, d_ff=d_ff, n_chunk=n_chunk,
                             has_mix=mix is not None, has_final=final_g is not None)
    return pl.pallas_call(
        kern,
        out_shape=jax.ShapeDtypeStruct((n, d), F32),
        grid=(n // tm,),
        in_specs=specs,
        out_specs=row(d),
        compiler_params=pltpu.CompilerParams(dimension_semantics=("arbitrary",),
                                             vmem_limit_bytes=VMEM_LIMIT),
        name="ffn_mix" if mix is not None else "ffn",
    )(*args)


def _rope(y, c, s):
    n = y.shape[-1]
    lane = lax.broadcasted_iota(jnp.int32, y.shape, 1)
    first = (lane & (HEAD_DIM - 1)) < HEAD_DIM // 2
    rot = jnp.where(first, pltpu.roll(y, n - HEAD_DIM // 2, 1), pltpu.roll(y, HEAD_DIM // 2, 1))
    return y * c + rot * s


def _proj_kernel(x_ref, g_ref, wqkv_ref, wqi_ref, wkw_ref, wconv_ref, wgb_ref, wb2_ref,
                 convw_ref, cprev_ref, cos_ref, sin_ref, coskw_ref, sinkw_ref,
                 kf_ref, vf_ref, kif_ref, kb_ref, kib_ref, q_ref, v_ref, qi_ref, wi_ref,
                 gbyb_ref, nconv_ref, ubuf, *, tm, transposed):
    t = pl.program_id(1)
    h = _rms(x_ref[0], g_ref[...]).astype(BF16)
    cosf, sinf = cos_ref[...], sin_ref[...]
    idx_w = IDX_HEADS * IDX_DIM

    q = _rope(_dot(h, wqkv_ref[:, 0:ATT_W]), cosf, sinf) * (HEAD_DIM ** -0.5)
    k = _rope(_dot(h, wqkv_ref[:, ATT_W:2 * ATT_W]), cosf, sinf)
    v = _dot(h, wqkv_ref[:, 2 * ATT_W:3 * ATT_W])
    qi = _rope(_dot(h, wqi_ref[...]), cosf[:, :idx_w], sinf[:, :idx_w])
    kw = _rope(_dot(h, wkw_ref[...]), coskw_ref[...], sinkw_ref[...])

    kf_ref[0] = k
    vf_ref[0] = v
    kif_ref[0] = kw[:, :IDX_DIM]
    kb_ref[0, 0] = k.astype(BF16)
    kib_ref[0, 0] = kw[:, :IDX_DIM].astype(BF16)
    if transposed:
        q_ref[0] = q.T.astype(BF16)
        v_ref[0, 0] = v.T.astype(BF16)
        qi_ref[0] = qi.T.astype(BF16)
        wi_ref[0] = kw.T[IDX_DIM:IDX_DIM + SUBLANES, :]
    else:
        q_ref[0] = q.astype(BF16)
        v_ref[0] = v.astype(BF16)
        qi_ref[0] = qi.astype(BF16)
        wi_ref[0] = kw

    cb = _dot(h, wconv_ref[:, 0:CONV_W])
    cc = _dot(h, wconv_ref[:, CONV_W:2 * CONV_W])
    cx = _dot(h, wconv_ref[:, 2 * CONV_W:3 * CONV_W])
    u = cc * cx

    @pl.when(t == 0)
    def _():
        ubuf[SUBLANES - 2:SUBLANES, :] = cprev_ref[0]

    ubuf[SUBLANES:SUBLANES + tm, :] = u
    um1 = ubuf[SUBLANES - 1:SUBLANES - 1 + tm, :]
    um2 = ubuf[SUBLANES - 2:SUBLANES - 2 + tm, :]
    cw = convw_ref[...]
    y_conv = cw[0:1, :] * um2 + cw[1:2, :] * um1 + cw[2:3, :] * u
    tail = u[tm - 2:tm, :]
    ubuf[SUBLANES - 2:SUBLANES, :] = tail
    nconv_ref[0] = tail

    y_b = _dot((cb * y_conv).astype(BF16), wb2_ref[...])
    g_b = jax.nn.sigmoid(_dot(h, wgb_ref[...]))
    gbyb_ref[0] = (g_b * y_b).astype(BF16)


def _proj_call(x, w, conv_prev, tabs, *, tm, transposed):
    b, t, d = x.shape
    nt = t // tm
    cos, sin, coskw, sinkw = tabs
    idx_w = IDX_HEADS * IDX_DIM
    consts = [w["mix_g"], w["wqkv"], w["wqi"], w["wkw"], w["wconv"], w["wgb"], w["wb2"], w["conv_w"]]
    tile3 = lambda width: pl.BlockSpec((1, tm, width), lambda i, j: (i, j, 0))
    tile4 = lambda width: pl.BlockSpec((1, 1, tm, width), lambda i, j: (i, j, 0, 0))
    tab = lambda a: pl.BlockSpec((tm, a.shape[1]), lambda i, j: (j, 0))
    in_specs = ([tile3(d)] + [_const_spec(c.shape) for c in consts]
                + [pl.BlockSpec((1, CONV_K - 1, CONV_W), lambda i, j: (i, 0, 0))]
                + [tab(cos), tab(sin), tab(coskw), tab(sinkw)])
    out_shape = [
        jax.ShapeDtypeStruct((b, t, ATT_W), F32),
        jax.ShapeDtypeStruct((b, t, ATT_W), F32),
        jax.ShapeDtypeStruct((b, t, IDX_DIM), F32),
        jax.ShapeDtypeStruct((b, nt, tm, ATT_W), BF16),
        jax.ShapeDtypeStruct((b, nt, tm, IDX_DIM), BF16),
    ]
    out_specs = [tile3(ATT_W), tile3(ATT_W), tile3(IDX_DIM), tile4(ATT_W), tile4(IDX_DIM)]
    if transposed:
        out_shape += [
            jax.ShapeDtypeStruct((b, ATT_W, t), BF16),
            jax.ShapeDtypeStruct((b, nt, ATT_W, tm), BF16),
            jax.ShapeDtypeStruct((b, idx_w, t), BF16),
            jax.ShapeDtypeStruct((b, SUBLANES, t), F32),
        ]
        out_specs += [
            pl.BlockSpec((1, ATT_W, tm), lambda i, j: (i, 0, j)),
            pl.BlockSpec((1, 1, ATT_W, tm), lambda i, j: (i, j, 0, 0)),
            pl.BlockSpec((1, idx_w, tm), lambda i, j: (i, 0, j)),
            pl.BlockSpec((1, SUBLANES, tm), lambda i, j: (i, 0, j)),
        ]
    else:
        out_shape += [
            jax.ShapeDtypeStruct((b, t, ATT_W), BF16),
            jax.ShapeDtypeStruct((b, t, ATT_W), BF16),
            jax.ShapeDtypeStruct((b, t, idx_w), BF16),
            jax.ShapeDtypeStruct((b, t, LANES), F32),
        ]
        out_specs += [tile3(ATT_W), tile3(ATT_W), tile3(idx_w), tile3(LANES)]
    out_shape += [
        jax.ShapeDtypeStruct((b, t, d), BF16),
        jax.ShapeDtypeStruct((b, CONV_K - 1, CONV_W), F32),
    ]
    out_specs += [tile3(d), pl.BlockSpec((1, CONV_K - 1, CONV_W), lambda i, j: (i, 0, 0))]
    kern = functools.partial(_proj_kernel, tm=tm, transposed=transposed)
    return pl.pallas_call(
        kern,
        out_shape=out_shape,
        grid=(b, nt),
        in_specs=in_specs,
        out_specs=out_specs,
        scratch_shapes=[pltpu.VMEM((tm + SUBLANES, CONV_W), F32)],
        compiler_params=pltpu.CompilerParams(dimension_semantics=("arbitrary", "arbitrary"),
                                             vmem_limit_bytes=VMEM_LIMIT),
        name="proj",
    )(x, *consts, conv_prev, cos, sin, coskw, sinkw)


def _slab_reduce(x, op):
    kb, qw = x.shape
    return op(x.reshape(kb // SUBLANES, SUBLANES, qw), axis=0)


def _attend_kernel(qi_ref, wi_ref, ki_ref, k_ref, q_ref, v_ref, o_ref, s_ref, lg_ref, ot_ref,
                   *, kb_rows, qw, n_kb_static, q_pos0, n_valid, top_k):
    j = pl.program_id(1)
    n_kb = (j + 1) if n_kb_static is None else n_kb_static
    kf = float(top_k)

    def over_blocks(body, init):
        return lax.fori_loop(0, n_kb, body, init)

    last = n_kb - 1
    row = lax.broadcasted_iota(jnp.int32, (kb_rows, qw), 0) + last * kb_rows
    qpos = lax.broadcasted_iota(jnp.int32, (kb_rows, qw), 1) + (q_pos0 + j * qw)
    chunk_shift = CHUNK.bit_length() - 1
    adm_last = ((row >> chunk_shift) <= (qpos >> chunk_shift)) & (row < n_valid)

    def score_block(kb):
        ki = ki_ref[0, kb]
        s = jnp.zeros((kb_rows, qw), F32)
        for hh in range(IDX_HEADS):
            d = _dot(ki, qi_ref[0, hh * IDX_DIM:(hh + 1) * IDX_DIM, :])
            s = s + jnp.maximum(d, 0.0) * wi_ref[0, hh:hh + 1, :]
        return s * IDX_SCALE

    def score_body(kb, carry):
        mx, mn = carry
        s = score_block(kb)
        s_ref[kb] = s
        return (jnp.maximum(mx, _slab_reduce(s, jnp.max)), jnp.minimum(mn, _slab_reduce(s, jnp.min)))

    mx8 = jnp.full((SUBLANES, qw), -BIG, F32)
    mn8 = jnp.full((SUBLANES, qw), BIG, F32)
    mx8, mn8 = lax.fori_loop(0, last, score_body, (mx8, mn8))
    s_last = score_block(last)
    s_ref[last] = jnp.where(adm_last, s_last, NEG)
    mx8 = jnp.maximum(mx8, _slab_reduce(jnp.where(adm_last, s_last, -BIG), jnp.max))
    mn8 = jnp.minimum(mn8, _slab_reduce(jnp.where(adm_last, s_last, BIG), jnp.min))
    smax = jnp.max(mx8, axis=0, keepdims=True)
    smin = jnp.min(mn8, axis=0, keepdims=True)
    n_adm = (jnp.sum(_slab_reduce(jnp.where(adm_last, 1.0, 0.0), jnp.sum), axis=0, keepdims=True)
             + jnp.float32(1.0) * (last * kb_rows))

    def count(pred_fn):
        def body(kb, acc):
            return acc + _slab_reduce(jnp.where(pred_fn(s_ref[kb]), 1.0, 0.0), jnp.sum)
        return jnp.sum(over_blocks(body, jnp.zeros((SUBLANES, qw), F32)), axis=0, keepdims=True)

    def bisect(_, carry):
        lo, hi, c_hi = carry
        mid = 0.5 * lo + 0.5 * hi
        cnt = count(lambda s: s >= mid)
        ge = cnt >= kf
        return jnp.where(ge, mid, lo), jnp.where(ge, hi, mid), jnp.where(ge, c_hi, cnt)

    hi0 = smax + (jnp.abs(smax) + 1.0) * (2.0 ** -10)
    _, hi, c_hi = lax.fori_loop(0, N_BISECT, bisect, (smin, hi0, jnp.zeros((1, qw), F32)))

    def snap(carry):
        hi, c_hi, done, tau, n_gt = carry

        def vbody(kb, acc):
            s = s_ref[kb]
            return jnp.maximum(acc, _slab_reduce(jnp.where(s < hi, s, -BIG), jnp.max))
        cand = jnp.max(over_blocks(vbody, jnp.full((SUBLANES, qw), -BIG, F32)), axis=0, keepdims=True)
        c_ge = count(lambda s: s >= cand)
        ok = c_ge >= kf
        fresh = ok & (done < 0.5)
        return (jnp.where(ok, hi, cand), jnp.where(ok, c_hi, c_ge), jnp.where(ok, 1.0, done),
                jnp.where(fresh, cand, tau), jnp.where(fresh, c_hi, n_gt))

    few = n_adm <= kf
    carry0 = (hi, c_hi, jnp.where(few, 1.0, 0.0), jnp.full((1, qw), 0.5 * NEG, F32), jnp.zeros((1, qw), F32))
    _, _, _, tau, n_gt = lax.while_loop(lambda c: jnp.min(c[2]) < 0.5, snap, carry0)
    need = kf - n_gt

    r_i = lax.broadcasted_iota(jnp.int32, (kb_rows, kb_rows), 0)
    c_i = lax.broadcasted_iota(jnp.int32, (kb_rows, kb_rows), 1)
    tri = jnp.where(c_i <= r_i, 1.0, 0.0).astype(BF16)

    def select_body(kb, off):
        s = s_ref[kb]
        eq = s == tau
        rank = _dot(tri, jnp.where(eq, 1.0, 0.0).astype(BF16))
        y = jnp.where(eq, rank + off, jnp.where(s > tau, 0.0, BIG))
        s_ref[kb] = jnp.where(y <= need, 0.0, NEG)
        return off + rank[kb_rows - 1:kb_rows, :]

    over_blocks(select_body, jnp.zeros((1, qw), F32))

    pair_w = 2 * HEAD_DIM
    pr = lax.broadcasted_iota(jnp.int32, (pair_w, qw), 0)
    q_pairs = []
    for pair in range(N_HEADS // 2):
        qp = q_ref[0, pair * pair_w:(pair + 1) * pair_w, :]
        zero = jnp.zeros_like(qp)
        q_pairs.append(jnp.concatenate([jnp.where(pr < HEAD_DIM, qp, zero),
                                        jnp.where(pr >= HEAD_DIM, qp, zero)], axis=1))

    def logit_body(kb, m8s):
        bias = s_ref[kb]
        out = []
        for pair in range(N_HEADS // 2):
            lg2 = _dot(k_ref[0, kb, :, pair * pair_w:(pair + 1) * pair_w], q_pairs[pair])
            for half in range(2):
                hd = 2 * pair + half
                lg = lg2[:, half * qw:(half + 1) * qw] + bias
                lg_ref[hd, kb] = lg
                out.append(jnp.maximum(m8s[hd], _slab_reduce(lg, jnp.max)))
        return tuple(out)

    m8s = over_blocks(logit_body, tuple(jnp.full((SUBLANES, qw), -BIG, F32) for _ in range(N_HEADS)))
    ms = [jnp.max(m8, axis=0, keepdims=True) for m8 in m8s]

    ot_ref[...] = jnp.zeros(ot_ref.shape, F32)

    def pv_body(kb, l8s):
        out = []
        for hd in range(N_HEADS):
            rows = slice(hd * HEAD_DIM, (hd + 1) * HEAD_DIM)
            e = jnp.exp(lg_ref[hd, kb] - ms[hd])
            ot_ref[rows, :] += _dot(v_ref[0, kb, rows, :], e.astype(BF16))
            out.append(l8s[hd] + _slab_reduce(e, jnp.sum))
        return tuple(out)

    l8s = over_blocks(pv_body, tuple(jnp.zeros((SUBLANES, qw), F32) for _ in range(N_HEADS)))
    for hd in range(N_HEADS):
        rows = slice(hd * HEAD_DIM, (hd + 1) * HEAD_DIM)
        ot_ref[rows, :] = ot_ref[rows, :] / jnp.sum(l8s[hd], axis=0, keepdims=True)

    o_ref[0] = ot_ref[...].T.astype(BF16)


def _attend_call(qi_t, wi_t, ki_b, k_b, q_t, v_tb, *, qw, n_kb_static, q_pos0, n_valid, top_k):
    b, n_kb_max, kb_rows, _ = k_b.shape
    t_q = q_t.shape[2]
    n_q = t_q // qw
    idx_w = IDX_HEADS * IDX_DIM
    batch_blk = lambda shape: pl.BlockSpec((1,) + shape, lambda i, j: (i,) + (0,) * len(shape))
    kern = functools.partial(_attend_kernel, kb_rows=kb_rows, qw=qw, n_kb_static=n_kb_static,
                             q_pos0=q_pos0, n_valid=n_valid, top_k=top_k)
    return pl.pallas_call(
        kern,
        out_shape=jax.ShapeDtypeStruct((b, t_q, ATT_W), BF16),
        grid=(b, n_q),
        in_specs=[
            pl.BlockSpec((1, idx_w, qw), lambda i, j: (i, 0, j)),
            pl.BlockSpec((1, SUBLANES, qw), lambda i, j: (i, 0, j)),
            batch_blk((n_kb_max, kb_rows, IDX_DIM)),
            batch_blk((n_kb_max, kb_rows, ATT_W)),
            pl.BlockSpec((1, ATT_W, qw), lambda i, j: (i, 0, j)),
            batch_blk((n_kb_max, ATT_W, kb_rows)),
        ],
        out_specs=pl.BlockSpec((1, qw, ATT_W), lambda i, j: (i, j, 0)),
        scratch_shapes=[pltpu.VMEM((n_kb_max, kb_rows, qw), F32),
                        pltpu.VMEM((N_HEADS, n_kb_max, kb_rows, qw), F32),
                        pltpu.VMEM((ATT_W, qw), F32)],
        compiler_params=pltpu.CompilerParams(dimension_semantics=("arbitrary", "arbitrary"),
                                             vmem_limit_bytes=VMEM_LIMIT),
        name="attend",
    )(qi_t, wi_t, ki_b, k_b, q_t, v_tb)


def _rope_tables(pos):
    half = HEAD_DIM // 2
    freqs = ROPE_THETA ** (-jnp.arange(half, dtype=F32) / half)
    ang = pos.astype(F32)[:, None] * freqs[None, :]
    cos, sin = jnp.cos(ang), jnp.sin(ang)
    cos_h = jnp.concatenate([cos, cos], axis=-1)
    sin_h = jnp.concatenate([-sin, sin], axis=-1)
    n = pos.shape[0]
    pad_c = jnp.ones((n, LANES - IDX_DIM), F32)
    pad_s = jnp.zeros((n, LANES - IDX_DIM), F32)
    return (jnp.tile(cos_h, (1, N_HEADS)), jnp.tile(sin_h, (1, N_HEADS)),
            jnp.concatenate([cos_h, pad_c], axis=-1), jnp.concatenate([sin_h, pad_s], axis=-1))


def _layer_weights(i, ffn1_norm, ffn1_up, ffn1_down, mix_norm, w_in, conv_w, w_branch, w_out,
                   ffn2_norm, ffn2_up, ffn2_down):
    d = w_in.shape[1]
    idx_w = IDX_HEADS * IDX_DIM
    o = 0
    wi = w_in[i]
    wqkv = wi[:, o:o + 3 * ATT_W]; o += 3 * ATT_W
    wqi = wi[:, o:o + idx_w]; o += idx_w
    wkw = wi[:, o:o + IDX_DIM + IDX_HEADS]; o += IDX_DIM + IDX_HEADS
    wconv = wi[:, o:o + 3 * CONV_W]; o += 3 * CONV_W
    wga = wi[:, o:o + d]; o += d
    wgb = wi[:, o:o + d]
    wkw = jnp.pad(wkw, ((0, 0), (0, LANES - IDX_DIM - IDX_HEADS)))
    c = lambda a: a.astype(BF16)
    return dict(
        ffn1_g=ffn1_norm[i][None], ffn1_up=c(ffn1_up[i]), ffn1_dn=c(ffn1_down[i]),
        mix_g=mix_norm[i][None], wqkv=c(wqkv), wqi=c(wqi), wkw=c(wkw), wconv=c(wconv),
        wga=c(wga), wgb=c(wgb), conv_w=conv_w[i],
        wb1=c(w_branch[i][:ATT_W]), wb2=c(w_branch[i][ATT_W:]), wout=c(w_out[i]),
        ffn2_g=ffn2_norm[i][None], ffn2_up=c(ffn2_up[i]), ffn2_dn=c(ffn2_down[i]),
    )


def _pick_tile(n, pref):
    t = min(pref, n)
    while n % t:
        t //= 2
    return t


def kernel(x_prompt, x_sample, cache_k, cache_v, cache_kidx, state_conv, ffn1_norm, ffn1_up, ffn1_down, mix_norm, w_in, conv_w, w_branch, w_out, ffn2_norm, ffn2_up, ffn2_down, final_norm):
    depth = w_in.shape[0]
    bp, tp, d = x_prompt.shape
    bs, ts, _ = x_sample.shape
    past = cache_k.shape[2]
    qw_p = 2 * LANES
    qw_s = LANES
    l_s = past + ts
    kb_s = -(-l_s // LANES) * LANES
    n_kb_s = 1
    pad_s = kb_s - l_s
    topk_p = min(TOPK_MAX, tp // 4)
    topk_s = min(TOPK_MAX, l_s // 4)

    tabs_p = _rope_tables(jnp.arange(tp))
    tabs_s = _rope_tables(past + jnp.arange(ts))
    fin = final_norm[None]
    zero_conv = jnp.zeros((bp, CONV_K - 1, CONV_W), F32)
    tm_p = _pick_tile(bp * tp, 512)
    tm_s = _pick_tile(bs * ts, 512)

    xp = x_prompt.reshape(bp * tp, d)
    xs = x_sample.reshape(bs * ts, d)
    outs = [[] for _ in range(8)]
    for i in range(depth):
        w = _layer_weights(i, ffn1_norm, ffn1_up, ffn1_down, mix_norm, w_in, conv_w, w_branch, w_out,
                           ffn2_norm, ffn2_up, ffn2_down)
        last = i == depth - 1
        mixw = lambda oatt, gbyb: (oatt, gbyb, w["mix_g"], w["wga"], w["wb1"], w["wout"])

        xp = _ffn_call(xp, w["ffn1_g"], w["ffn1_up"], w["ffn1_dn"], tm=tm_p)
        (kf, vf, kif, kb, kib, qt, vtb, qit, wit, gbyb, nconv) = _proj_call(
            xp.reshape(bp, tp, d), w, zero_conv, tabs_p, tm=qw_p, transposed=True)
        oatt = _attend_call(qit, wit, kib, kb, qt, vtb, qw=qw_p, n_kb_static=None,
                            q_pos0=0, n_valid=tp, top_k=topk_p)
        xp = _ffn_call(xp, w["ffn2_g"], w["ffn2_up"], w["ffn2_dn"],
                       mix=mixw(oatt.reshape(bp * tp, ATT_W), gbyb.reshape(bp * tp, d)),
                       final_g=fin if last else None, tm=tm_p)
        for lst, a in zip(outs[:4], (kf, vf, kif, nconv)):
            lst.append(a)

        xs = _ffn_call(xs, w["ffn1_g"], w["ffn1_up"], w["ffn1_dn"], tm=tm_s)
        (kf, vf, kif, kb, kib, qn, vn, qin, win, gbyb, nconv) = _proj_call(
            xs.reshape(bs, ts, d), w, state_conv[i], tabs_s, tm=ts, transposed=False)
        lane_pad = lambda a: jnp.pad(jnp.swapaxes(a, 1, 2), ((0, 0), (0, 0), (0, qw_s - ts)))
        key_blocks = lambda past_rows, new_rows: jnp.pad(
            jnp.concatenate([past_rows.astype(BF16), new_rows], axis=1),
            ((0, 0), (0, pad_s), (0, 0))).reshape(bs, n_kb_s, kb_s, -1)
        k_all = key_blocks(cache_k[i].reshape(bs, past, ATT_W), kb[:, 0])
        ki_all = key_blocks(cache_kidx[i], kib[:, 0])
        v_all = jnp.swapaxes(key_blocks(cache_v[i].reshape(bs, past, ATT_W), vn), 2, 3)
        oatt = _attend_call(lane_pad(qin), lane_pad(win[:, :, IDX_DIM:IDX_DIM + SUBLANES]), ki_all, k_all,
                            lane_pad(qn), v_all, qw=qw_s, n_kb_static=n_kb_s,
                            q_pos0=past, n_valid=l_s, top_k=topk_s)[:, :ts]
        xs = _ffn_call(xs, w["ffn2_g"], w["ffn2_up"], w["ffn2_dn"],
                       mix=mixw(oatt.reshape(bs * ts, ATT_W), gbyb.reshape(bs * ts, d)),
                       final_g=fin if last else None, tm=tm_s)
        for lst, a in zip(outs[4:], (kf, vf, kif, nconv)):
            lst.append(a)

    heads = lambda a: a.reshape(a.shape[:-1] + (N_HEADS, HEAD_DIM))
    st = [jnp.stack(l) for l in outs]
    return (xp.reshape(bp, tp, d), xs.reshape(bs, ts, d),
            heads(st[0]), heads(st[1]), st[2], st[3],
            heads(st[4]), heads(st[5]), st[6], st[7])
```

```python
import functools

import jax
import jax.numpy as jnp
from jax import lax
from jax.experimental import pallas as pl
from jax.experimental.pallas import tpu as pltpu

CHUNK = 64
N_HEADS = 8
HEAD_DIM = 64
ATT_W = N_HEADS * HEAD_DIM
IDX_HEADS = 4
IDX_DIM = 64
IDX_W = IDX_HEADS * IDX_DIM
TOPK_MAX = 256
CONV_W = 512
CONV_K = 3
ROPE_THETA = 10000.0
EPS = 1e-6
NEG = -1e30
BIG = 3e38
IDX_SCALE = (IDX_DIM ** -0.5) * (IDX_HEADS ** -0.5)

LANES = 128
SUBLANES = 8
VMEM_LIMIT = 56 * 1024 * 1024
N_BISECT = 20

BF16 = jnp.bfloat16
F32 = jnp.float32


def _dot(a, b):
    return jnp.dot(a, b, preferred_element_type=F32)


def _rms(x, g):
    return x * lax.rsqrt(jnp.mean(x * x, axis=-1, keepdims=True) + EPS) * g


def _const_spec(shape):
    nd = len(shape)
    return pl.BlockSpec(shape, lambda *_: (0,) * nd, pipeline_mode=pl.Buffered(1))


def _layer_spec(a, layer):
    nd = a.ndim
    return pl.BlockSpec((None,) + a.shape[1:], lambda *_: (layer,) + (0,) * (nd - 1),
                        pipeline_mode=pl.Buffered(1))


def _ffn_kernel(*refs, d_ff, n_chunk, has_mix, has_final):
    it = iter(refs)
    x_ref = next(it)
    if has_mix:
        oatt_ref, gbyb_ref, mixg_ref, wga_ref, wb1_ref, wout_ref = (next(it) for _ in range(6))
    g_ref, wup_ref, wdn_ref = next(it), next(it), next(it)
    fin_ref = next(it) if has_final else None
    o_ref = next(it)

    x = x_ref[...]
    if has_mix:
        hm = _rms(x, mixg_ref[...]).astype(BF16)
        g_a = jax.nn.sigmoid(_dot(hm, wga_ref[...]))
        y_a = _dot(oatt_ref[...], wb1_ref[...])
        mixed = g_a * y_a + gbyb_ref[...].astype(F32)
        x = x + _dot(mixed.astype(BF16), wout_ref[...])
    h = _rms(x, g_ref[...]).astype(BF16)
    cw = d_ff // n_chunk
    acc = jnp.zeros(x.shape, F32)
    for c in range(n_chunk):
        a = _dot(h, wup_ref[:, c * cw:(c + 1) * cw])
        b = _dot(h, wup_ref[:, d_ff + c * cw:d_ff + (c + 1) * cw])
        act = (a * jax.nn.sigmoid(a) * b).astype(BF16)
        acc = acc + _dot(act, wdn_ref[c * cw:(c + 1) * cw, :])
    y = x + 0.5 * acc
    if has_final:
        y = _rms(y, fin_ref[...])
    o_ref[...] = y


def _ffn_call(x, w, layer, which, *, mix=None, final_g=None, tm):
    n, d = x.shape
    g, w_up, w_dn = w[which + "_g"], w[which + "_up"], w[which + "_dn"]
    d_ff = w_dn.shape[1]
    n_chunk = 2 if d_ff % (2 * LANES) == 0 else 1
    row = lambda width: pl.BlockSpec((tm, width), lambda i: (i, 0))
    args, specs = [x], [row(d)]
    if mix is not None:
        oatt, gbyb = mix
        consts = [w["mix_g"], w["wga"], w["wb1"], w["wout"]]
        args += [oatt, gbyb] + consts
        specs += [row(oatt.shape[1]), row(d)] + [_layer_spec(c, layer) for c in consts]
    args += [g, w_up, w_dn]
    specs += [_layer_spec(c, layer) for c in (g, w_up, w_dn)]
    if final_g is not None:
        args.append(final_g)
        specs.append(_const_spec(final_g.shape))
    kern = functools.partial(_ffn_kernel, d_ff=d_ff, n_chunk=n_chunk,
                             has_mix=mix is not None, has_final=final_g is not None)
    return pl.pallas_call(
        kern,
        out_shape=jax.ShapeDtypeStruct((n, d), F32),
        grid=(n // tm,),
        in_specs=specs,
        out_specs=row(d),
        compiler_params=pltpu.CompilerParams(dimension_semantics=("arbitrary",),
                                             vmem_limit_bytes=VMEM_LIMIT),
        name="ffn_mix" if mix is not None else "ffn",
    )(*args)


def _rope(y, c, s):
    n = y.shape[-1]
    lane = lax.broadcasted_iota(jnp.int32, y.shape, 1)
    first = (lane & (HEAD_DIM - 1)) < HEAD_DIM // 2
    rot = jnp.where(first, pltpu.roll(y, n - HEAD_DIM // 2, 1), pltpu.roll(y, HEAD_DIM // 2, 1))
    return y * c + rot * s


def _proj_kernel(x_ref, g_ref, wqkv_ref, wqi_ref, wkw_ref, wconv_ref, wgb_ref, wb2_ref,
                 convw_ref, cprev_ref, cos_ref, sin_ref, coskw_ref, sinkw_ref,
                 kf_ref, vf_ref, kif_ref, kb_ref, kib_ref, q_ref, v_ref, qi_ref, wi_ref,
                 gbyb_ref, nconv_ref, ubuf, *, tm, transposed):
    t = pl.program_id(1)
    h = _rms(x_ref[0], g_ref[...]).astype(BF16)
    cosf, sinf = cos_ref[...], sin_ref[...]

    q = _rope(_dot(h, wqkv_ref[:, 0:ATT_W]), cosf, sinf) * (HEAD_DIM ** -0.5)
    k = _rope(_dot(h, wqkv_ref[:, ATT_W:2 * ATT_W]), cosf, sinf)
    v = _dot(h, wqkv_ref[:, 2 * ATT_W:3 * ATT_W])
    qi = _rope(_dot(h, wqi_ref[...]), cosf[:, :IDX_W], sinf[:, :IDX_W])
    kw = _rope(_dot(h, wkw_ref[...]), coskw_ref[...], sinkw_ref[...])

    kf_ref[0] = k
    vf_ref[0] = v
    kif_ref[0] = kw[:, :IDX_DIM]
    kb_ref[0, 0] = k.astype(BF16)
    kib_ref[0, 0] = kw[:, :IDX_DIM].astype(BF16)
    if transposed:
        q_ref[0] = q.T.astype(BF16)
        v_ref[0, 0] = v.T.astype(BF16)
        qi_ref[0] = qi.T.astype(BF16)
        wi_ref[0] = kw.T[IDX_DIM:IDX_DIM + SUBLANES, :]
    else:
        q_ref[0] = q.astype(BF16)
        v_ref[0] = v.astype(BF16)
        qi_ref[0] = qi.astype(BF16)
        wi_ref[0] = kw

    cb = _dot(h, wconv_ref[:, 0:CONV_W])
    cc = _dot(h, wconv_ref[:, CONV_W:2 * CONV_W])
    cx = _dot(h, wconv_ref[:, 2 * CONV_W:3 * CONV_W])
    u = cc * cx

    @pl.when(t == 0)
    def _():
        ubuf[SUBLANES - 2:SUBLANES, :] = cprev_ref[0]

    ubuf[SUBLANES:SUBLANES + tm, :] = u
    um1 = ubuf[SUBLANES - 1:SUBLANES - 1 + tm, :]
    um2 = ubuf[SUBLANES - 2:SUBLANES - 2 + tm, :]
    cw = convw_ref[...]
    y_conv = cw[0:1, :] * um2 + cw[1:2, :] * um1 + cw[2:3, :] * u
    tail = u[tm - 2:tm, :]
    ubuf[SUBLANES - 2:SUBLANES, :] = tail
    nconv_ref[0] = tail

    y_b = _dot((cb * y_conv).astype(BF16), wb2_ref[...])
    g_b = jax.nn.sigmoid(_dot(h, wgb_ref[...]))
    gbyb_ref[0] = (g_b * y_b).astype(BF16)


def _proj_call(x, w, layer, conv_prev, tabs, *, tm, transposed):
    b, t, d = x.shape
    nt = t // tm
    cos, sin, coskw, sinkw = tabs
    consts = [w["mix_g"], w["wqkv"], w["wqi"], w["wkw"], w["wconv"], w["wgb"], w["wb2"], w["conv_w"]]
    tile3 = lambda width: pl.BlockSpec((1, tm, width), lambda i, j: (i, j, 0))
    tile4 = lambda width: pl.BlockSpec((1, 1, tm, width), lambda i, j: (i, j, 0, 0))
    tab = lambda a: pl.BlockSpec((tm, a.shape[1]), lambda i, j: (j, 0))
    in_specs = ([tile3(d)] + [_layer_spec(c, layer) for c in consts]
                + [pl.BlockSpec((1, CONV_K - 1, CONV_W), lambda i, j: (i, 0, 0))]
                + [tab(cos), tab(sin), tab(coskw), tab(sinkw)])
    out_shape = [
        jax.ShapeDtypeStruct((b, t, ATT_W), F32),
        jax.ShapeDtypeStruct((b, t, ATT_W), F32),
        jax.ShapeDtypeStruct((b, t, IDX_DIM), F32),
        jax.ShapeDtypeStruct((b, nt, tm, ATT_W), BF16),
        jax.ShapeDtypeStruct((b, nt, tm, IDX_DIM), BF16),
    ]
    out_specs = [tile3(ATT_W), tile3(ATT_W), tile3(IDX_DIM), tile4(ATT_W), tile4(IDX_DIM)]
    if transposed:
        out_shape += [
            jax.ShapeDtypeStruct((b, ATT_W, t), BF16),
            jax.ShapeDtypeStruct((b, nt, ATT_W, tm), BF16),
            jax.ShapeDtypeStruct((b, IDX_W, t), BF16),
            jax.ShapeDtypeStruct((b, SUBLANES, t), F32),
        ]
        out_specs += [
            pl.BlockSpec((1, ATT_W, tm), lambda i, j: (i, 0, j)),
            pl.BlockSpec((1, 1, ATT_W, tm), lambda i, j: (i, j, 0, 0)),
            pl.BlockSpec((1, IDX_W, tm), lambda i, j: (i, 0, j)),
            pl.BlockSpec((1, SUBLANES, tm), lambda i, j: (i, 0, j)),
        ]
    else:
        out_shape += [
            jax.ShapeDtypeStruct((b, t, ATT_W), BF16),
            jax.ShapeDtypeStruct((b, t, ATT_W), BF16),
            jax.ShapeDtypeStruct((b, t, IDX_W), BF16),
            jax.ShapeDtypeStruct((b, t, LANES), F32),
        ]
        out_specs += [tile3(ATT_W), tile3(ATT_W), tile3(IDX_W), tile3(LANES)]
    out_shape += [
        jax.ShapeDtypeStruct((b, t, d), BF16),
        jax.ShapeDtypeStruct((b, CONV_K - 1, CONV_W), F32),
    ]
    out_specs += [tile3(d), pl.BlockSpec((1, CONV_K - 1, CONV_W), lambda i, j: (i, 0, 0))]
    kern = functools.partial(_proj_kernel, tm=tm, transposed=transposed)
    return pl.pallas_call(
        kern,
        out_shape=out_shape,
        grid=(b, nt),
        in_specs=in_specs,
        out_specs=out_specs,
        scratch_shapes=[pltpu.VMEM((tm + SUBLANES, CONV_W), F32)],
        compiler_params=pltpu.CompilerParams(dimension_semantics=("arbitrary", "arbitrary"),
                                             vmem_limit_bytes=VMEM_LIMIT),
        name="proj",
    )(x, *consts, conv_prev, cos, sin, coskw, sinkw)


def _slab_reduce(x, op):
    kb, qw = x.shape
    return op(x.reshape(kb // SUBLANES, SUBLANES, qw), axis=0)


def _attend_kernel(*refs, kb_rows, qw, n_kb_static, q_pos0, n_valid, top_k, assemble):
    j = pl.program_id(1)
    n_kb = (j + 1) if n_kb_static is None else n_kb_static
    kf = float(top_k)

    if assemble:
        (qi_ref, wi_ref, kic_ref, kin_ref, kc_ref, kn_ref, q_ref, vtc_ref, vtn_ref,
         o_ref, s_ref, lg_ref, ot_ref, ki_s, k_s, v_s) = refs
        past, new = kc_ref.shape[0], kn_ref.shape[0]
        for dst, cached, fresh in ((ki_s, kic_ref, kin_ref), (k_s, kc_ref, kn_ref)):
            dst[0, 0:past, :] = cached[...]
            dst[0, past:past + new, :] = fresh[...]
            dst[0, past + new:, :] = jnp.zeros((kb_rows - past - new, dst.shape[2]), BF16)
        v_s[0, :, 0:past] = vtc_ref[...]
        v_s[0, :, past:] = vtn_ref[...]
        ki_blk = lambda kb: ki_s[kb]
        k_blk = lambda kb, lanes: k_s[kb, :, lanes]
        v_blk = lambda kb, rows: v_s[kb, rows, :]
    else:
        qi_ref, wi_ref, ki_ref, k_ref, q_ref, v_ref, o_ref, s_ref, lg_ref, ot_ref = refs
        ki_blk = lambda kb: ki_ref[0, kb]
        k_blk = lambda kb, lanes: k_ref[0, kb, :, lanes]
        v_blk = lambda kb, rows: v_ref[0, kb, rows, :]

    def over_blocks(body, init, n=None):
        n = n_kb if n is None else n
        pairs = lax.fori_loop(0, n // 2, lambda i, c: body([2 * i, 2 * i + 1], c), init)
        if isinstance(n, int):
            return body([n - 1], pairs) if n % 2 else pairs
        return lax.cond(n % 2 == 1, lambda c: body([n - 1], c), lambda c: c, pairs)

    last = n_kb - 1
    row = lax.broadcasted_iota(jnp.int32, (kb_rows, qw), 0) + last * kb_rows
    qpos = lax.broadcasted_iota(jnp.int32, (kb_rows, qw), 1) + (q_pos0 + j * qw)
    chunk_shift = CHUNK.bit_length() - 1
    adm_last = ((row >> chunk_shift) <= (qpos >> chunk_shift)) & (row < n_valid)

    def score_block(kb):
        ki = ki_blk(kb)
        s = jnp.zeros((kb_rows, qw), F32)
        for hh in range(IDX_HEADS):
            d = _dot(ki, qi_ref[0, hh * IDX_DIM:(hh + 1) * IDX_DIM, :])
            s = s + jnp.maximum(d, 0.0) * wi_ref[0, hh:hh + 1, :]
        return s * IDX_SCALE

    def score_body(kbs, carry):
        mx, mn = carry
        for kb in kbs:
            s = score_block(kb)
            s_ref[kb] = s
            mx = jnp.maximum(mx, _slab_reduce(s, jnp.max))
            mn = jnp.minimum(mn, _slab_reduce(s, jnp.min))
        return mx, mn

    mx8 = jnp.full((SUBLANES, qw), -BIG, F32)
    mn8 = jnp.full((SUBLANES, qw), BIG, F32)
    mx8, mn8 = over_blocks(score_body, (mx8, mn8), n=last)
    s_last = score_block(last)
    s_ref[last] = jnp.where(adm_last, s_last, NEG)
    mx8 = jnp.maximum(mx8, _slab_reduce(jnp.where(adm_last, s_last, -BIG), jnp.max))
    mn8 = jnp.minimum(mn8, _slab_reduce(jnp.where(adm_last, s_last, BIG), jnp.min))
    smax = jnp.max(mx8, axis=0, keepdims=True)
    smin = jnp.min(mn8, axis=0, keepdims=True)
    n_adm = (jnp.sum(_slab_reduce(jnp.where(adm_last, 1.0, 0.0), jnp.sum), axis=0, keepdims=True)
             + jnp.float32(1.0) * (last * kb_rows))

    def count(pred_fn):
        def body(kbs, acc):
            for kb in kbs:
                acc = acc + _slab_reduce(jnp.where(pred_fn(s_ref[kb]), 1.0, 0.0), jnp.sum)
            return acc
        return jnp.sum(over_blocks(body, jnp.zeros((SUBLANES, qw), F32)), axis=0, keepdims=True)

    def bisect(_, carry):
        lo, hi, c_hi = carry
        mid = 0.5 * lo + 0.5 * hi
        cnt = count(lambda s: s >= mid)
        ge = cnt >= kf
        return jnp.where(ge, mid, lo), jnp.where(ge, hi, mid), jnp.where(ge, c_hi, cnt)

    hi0 = smax + (jnp.abs(smax) + 1.0) * (2.0 ** -10)
    _, hi, c_hi = lax.fori_loop(0, N_BISECT, bisect, (smin, hi0, jnp.zeros((1, qw), F32)))

    def snap(carry):
        hi, c_hi, done, tau, n_gt = carry

        def vbody(kbs, acc):
            for kb in kbs:
                s = s_ref[kb]
                acc = jnp.maximum(acc, _slab_reduce(jnp.where(s < hi, s, -BIG), jnp.max))
            return acc
        cand = jnp.max(over_blocks(vbody, jnp.full((SUBLANES, qw), -BIG, F32)), axis=0, keepdims=True)
        c_ge = count(lambda s: s >= cand)
        ok = c_ge >= kf
        fresh = ok & (done < 0.5)
        return (jnp.where(ok, hi, cand), jnp.where(ok, c_hi, c_ge), jnp.where(ok, 1.0, done),
                jnp.where(fresh, cand, tau), jnp.where(fresh, c_hi, n_gt))

    few = n_adm <= kf
    carry0 = (hi, c_hi, jnp.where(few, 1.0, 0.0), jnp.full((1, qw), 0.5 * NEG, F32), jnp.zeros((1, qw), F32))
    _, _, _, tau, n_gt = lax.while_loop(lambda c: jnp.min(c[2]) < 0.5, snap, carry0)
    need = kf - n_gt

    r_i = lax.broadcasted_iota(jnp.int32, (kb_rows, kb_rows), 0)
    c_i = lax.broadcasted_iota(jnp.int32, (kb_rows, kb_rows), 1)
    tri = jnp.where(c_i <= r_i, 1.0, 0.0).astype(BF16)

    def select_body(kbs, off):
        for kb in kbs:
            s = s_ref[kb]
            eq = s == tau
            rank = _dot(tri, jnp.where(eq, 1.0, 0.0).astype(BF16))
            y = jnp.where(eq, rank + off, jnp.where(s > tau, 0.0, BIG))
            s_ref[kb] = jnp.where(y <= need, 0.0, NEG)
            off = off + rank[kb_rows - 1:kb_rows, :]
        return off

    over_blocks(select_body, jnp.zeros((1, qw), F32))

    pair_w = 2 * HEAD_DIM
    pr = lax.broadcasted_iota(jnp.int32, (pair_w, qw), 0)
    q_pairs = []
    for pair in range(N_HEADS // 2):
        qp = q_ref[0, pair * pair_w:(pair + 1) * pair_w, :]
        zero = jnp.zeros_like(qp)
        q_pairs.append(jnp.concatenate([jnp.where(pr < HEAD_DIM, qp, zero),
                                        jnp.where(pr >= HEAD_DIM, qp, zero)], axis=1))

    ot_ref[...] = jnp.zeros(ot_ref.shape, F32)
    head_rows = [slice(hd * HEAD_DIM, (hd + 1) * HEAD_DIM) for hd in range(N_HEADS)]

    def logit_body(kbs, m8s):
        m8s = list(m8s)
        for kb in kbs:
            bias = s_ref[kb]
            for pair in range(N_HEADS // 2):
                lg2 = _dot(k_blk(kb, slice(pair * pair_w, (pair + 1) * pair_w)), q_pairs[pair])
                for half in range(2):
                    hd = 2 * pair + half
                    lg = lg2[:, half * qw:(half + 1) * qw] + bias
                    lg_ref[hd, kb] = lg
                    m8s[hd] = jnp.maximum(m8s[hd], _slab_reduce(lg, jnp.max))
        return tuple(m8s)

    m8s = over_blocks(logit_body, tuple(jnp.full((SUBLANES, qw), -BIG, F32) for _ in range(N_HEADS)))
    ms = [jnp.max(m8, axis=0, keepdims=True) for m8 in m8s]

    def pv_body(kbs, l8s):
        l8s = list(l8s)
        for hd in range(N_HEADS):
            rows = head_rows[hd]
            acc = ot_ref[rows, :]
            for kb in kbs:
                e = jnp.exp(lg_ref[hd, kb] - ms[hd])
                acc = acc + _dot(v_blk(kb, rows), e.astype(BF16))
                l8s[hd] = l8s[hd] + _slab_reduce(e, jnp.sum)
            ot_ref[rows, :] = acc
        return tuple(l8s)

    l8s = over_blocks(pv_body, tuple(jnp.zeros((SUBLANES, qw), F32) for _ in range(N_HEADS)))
    for hd in range(N_HEADS):
        rows = head_rows[hd]
        ot_ref[rows, :] = ot_ref[rows, :] / jnp.sum(l8s[hd], axis=0, keepdims=True)

    o_ref[0] = ot_ref[...].T.astype(BF16)


def _attend_call(qi_t, wi_t, q_t, keys, *, layer=None, qw, kb_rows, n_kb_max, n_kb_static,
                 q_pos0, n_valid, top_k):
    b, _, t_q = q_t.shape
    n_q = t_q // qw
    assemble = layer is not None
    qspec = lambda rows: pl.BlockSpec((1, rows, qw), lambda i, j: (i, 0, j))
    scratch = [pltpu.VMEM((n_kb_max, kb_rows, qw), F32), pltpu.VMEM((N_HEADS, n_kb_max, kb_rows, qw), F32),
               pltpu.VMEM((ATT_W, qw), F32)]
    if assemble:
        kic, kc, vtc, kin, kn, vtn = keys
        cache = lambda a: pl.BlockSpec((None, None) + a.shape[2:], lambda i, j: (layer, i, 0, 0))
        fresh = lambda a: pl.BlockSpec((None,) + a.shape[1:], lambda i, j: (i, 0, 0))
        args = [qi_t, wi_t, kic, kin, kc, kn, q_t, vtc, vtn]
        in_specs = [qspec(IDX_W), qspec(SUBLANES), cache(kic), fresh(kin), cache(kc), fresh(kn),
                    qspec(ATT_W), cache(vtc), fresh(vtn)]
        scratch += [pltpu.VMEM((1, kb_rows, IDX_DIM), BF16), pltpu.VMEM((1, kb_rows, ATT_W), BF16),
                    pltpu.VMEM((1, ATT_W, kb_rows), BF16)]
    else:
        ki_b, k_b, v_tb = keys
        batch_blk = lambda a: pl.BlockSpec((1,) + a.shape[1:], lambda i, j: (i, 0, 0, 0))
        args = [qi_t, wi_t, ki_b, k_b, q_t, v_tb]
        in_specs = [qspec(IDX_W), qspec(SUBLANES), batch_blk(ki_b), batch_blk(k_b), qspec(ATT_W),
                    batch_blk(v_tb)]
    kern = functools.partial(_attend_kernel, kb_rows=kb_rows, qw=qw, n_kb_static=n_kb_static,
                             q_pos0=q_pos0, n_valid=n_valid, top_k=top_k, assemble=assemble)
    return pl.pallas_call(
        kern,
        out_shape=jax.ShapeDtypeStruct((b, t_q, ATT_W), BF16),
        grid=(b, n_q),
        in_specs=in_specs,
        out_specs=pl.BlockSpec((1, qw, ATT_W), lambda i, j: (i, j, 0)),
        scratch_shapes=scratch,
        compiler_params=pltpu.CompilerParams(dimension_semantics=("arbitrary", "arbitrary"),
                                             vmem_limit_bytes=VMEM_LIMIT),
        name="attend_cached" if assemble else "attend",
    )(*args)


def _rope_tables(pos):
    half = HEAD_DIM // 2
    freqs = ROPE_THETA ** (-jnp.arange(half, dtype=F32) / half)
    ang = pos.astype(F32)[:, None] * freqs[None, :]
    cos, sin = jnp.cos(ang), jnp.sin(ang)
    cos_h = jnp.concatenate([cos, cos], axis=-1)
    sin_h = jnp.concatenate([-sin, sin], axis=-1)
    n = pos.shape[0]
    pad_c = jnp.ones((n, LANES - IDX_DIM), F32)
    pad_s = jnp.zeros((n, LANES - IDX_DIM), F32)
    return (jnp.tile(cos_h, (1, N_HEADS)), jnp.tile(sin_h, (1, N_HEADS)),
            jnp.concatenate([cos_h, pad_c], axis=-1), jnp.concatenate([sin_h, pad_s], axis=-1))


def _prep_weights(ffn1_norm, ffn1_up, ffn1_down, mix_norm, w_in, conv_w, w_branch, w_out,
                  ffn2_norm, ffn2_up, ffn2_down):
    d = w_in.shape[1]
    c = lambda a: a.astype(BF16)
    gain = lambda a: a[:, None, :]
    sizes = (3 * ATT_W, IDX_W, IDX_DIM + IDX_HEADS, 3 * CONV_W, d, d)
    offs = [0]
    for s in sizes:
        offs.append(offs[-1] + s)
    wqkv, wqi, wkw, wconv, wga, wgb = (w_in[:, :, offs[n]:offs[n + 1]] for n in range(len(sizes)))
    wkw = jnp.pad(wkw, ((0, 0), (0, 0), (0, LANES - IDX_DIM - IDX_HEADS)))
    return dict(
        ffn1_g=gain(ffn1_norm), ffn1_up=c(ffn1_up), ffn1_dn=c(ffn1_down),
        mix_g=gain(mix_norm), wqkv=c(wqkv), wqi=c(wqi), wkw=c(wkw), wconv=c(wconv),
        wga=c(wga), wgb=c(wgb), conv_w=conv_w,
        wb1=c(w_branch[:, :ATT_W]), wb2=c(w_branch[:, ATT_W:]), wout=c(w_out),
        ffn2_g=gain(ffn2_norm), ffn2_up=c(ffn2_up), ffn2_dn=c(ffn2_down),
    )


def _pick_tile(n, pref):
    t = min(pref, n)
    while n % t:
        t //= 2
    return t


def kernel(x_prompt, x_sample, cache_k, cache_v, cache_kidx, state_conv, ffn1_norm, ffn1_up, ffn1_down, mix_norm, w_in, conv_w, w_branch, w_out, ffn2_norm, ffn2_up, ffn2_down, final_norm):
    depth = w_in.shape[0]
    bp, tp, d = x_prompt.shape
    bs, ts, _ = x_sample.shape
    past = cache_k.shape[2]
    assert past % LANES == 0 and ts % (2 * SUBLANES) == 0
    qw_p = 2 * LANES
    qw_s = LANES
    l_s = past + ts
    kb_s = -(-l_s // LANES) * LANES
    topk_p = min(TOPK_MAX, tp // 4)
    topk_s = min(TOPK_MAX, l_s // 4)

    w = _prep_weights(ffn1_norm, ffn1_up, ffn1_down, mix_norm, w_in, conv_w, w_branch, w_out,
                      ffn2_norm, ffn2_up, ffn2_down)
    tabs_p = _rope_tables(jnp.arange(tp))
    tabs_s = _rope_tables(past + jnp.arange(ts))
    fin = final_norm[None]
    zero_conv = jnp.zeros((bp, CONV_K - 1, CONV_W), F32)
    tm_p = _pick_tile(bp * tp, 512)
    tm_s = _pick_tile(bs * ts, 512)

    kc_all = cache_k.reshape(depth, bs, past, ATT_W).astype(BF16)
    vtc_all = jnp.swapaxes(cache_v.reshape(depth, bs, past, ATT_W), 2, 3).astype(BF16)
    kic_all = cache_kidx.astype(BF16)

    xp = x_prompt.reshape(bp * tp, d)
    xs = x_sample.reshape(bs * ts, d)
    outs = [[] for _ in range(8)]
    for i in range(depth):
        last = i == depth - 1

        xp = _ffn_call(xp, w, i, "ffn1", tm=tm_p)
        (kf, vf, kif, kb, kib, qt, vtb, qit, wit, gbyb, nconv) = _proj_call(
            xp.reshape(bp, tp, d), w, i, zero_conv, tabs_p, tm=qw_p, transposed=True)
        oatt = _attend_call(qit, wit, qt, (kib, kb, vtb), qw=qw_p, kb_rows=qw_p, n_kb_max=tp // qw_p,
                            n_kb_static=None, q_pos0=0, n_valid=tp, top_k=topk_p)
        xp = _ffn_call(xp, w, i, "ffn2", mix=(oatt.reshape(bp * tp, ATT_W), gbyb.reshape(bp * tp, d)),
                       final_g=fin if last else None, tm=tm_p)
        for lst, a in zip(outs[:4], (kf, vf, kif, nconv)):
            lst.append(a)

        xs = _ffn_call(xs, w, i, "ffn1", tm=tm_s)
        (kf, vf, kif, kb, kib, qn, vn, qin, win, gbyb, nconv) = _proj_call(
            xs.reshape(bs, ts, d), w, i, state_conv[i], tabs_s, tm=ts, transposed=False)
        lane_pad = lambda a, width: jnp.pad(jnp.swapaxes(a, 1, 2), ((0, 0), (0, 0), (0, width - ts)))
        oatt = _attend_call(lane_pad(qin, qw_s), lane_pad(win[:, :, IDX_DIM:IDX_DIM + SUBLANES], qw_s),
                            lane_pad(qn, qw_s),
                            (kic_all, kc_all, vtc_all, kib[:, 0], kb[:, 0], lane_pad(vn, kb_s - past)),
                            layer=i, qw=qw_s, kb_rows=kb_s, n_kb_max=1, n_kb_static=1,
                            q_pos0=past, n_valid=l_s, top_k=topk_s)[:, :ts]
        xs = _ffn_call(xs, w, i, "ffn2", mix=(oatt.reshape(bs * ts, ATT_W), gbyb.reshape(bs * ts, d)),
                       final_g=fin if last else None, tm=tm_s)
        for lst, a in zip(outs[4:], (kf, vf, kif, nconv)):
            lst.append(a)

    heads = lambda a: a.reshape(a.shape[:-1] + (N_HEADS, HEAD_DIM))
    st = [jnp.stack(l) for l in outs]
    return (xp.reshape(bp, tp, d), xs.reshape(bs, ts, d),
            heads(st[0]), heads(st[1]), st[2], st[3],
            heads(st[4]), heads(st[5]), st[6], st[7])
```

```python
import functools

import jax
import jax.numpy as jnp
from jax import lax
from jax.experimental import pallas as pl
from jax.experimental.pallas import tpu as pltpu

CHUNK = 64
N_HEADS = 8
HEAD_DIM = 64
ATT_W = N_HEADS * HEAD_DIM
IDX_HEADS = 4
IDX_DIM = 64
IDX_W = IDX_HEADS * IDX_DIM
TOPK_MAX = 256
CONV_W = 512
CONV_K = 3
ROPE_THETA = 10000.0
EPS = 1e-6
NEG = -1e30
BIG = 3e38
IDX_SCALE = (IDX_DIM ** -0.5) * (IDX_HEADS ** -0.5)

LANES = 128
SUBLANES = 8
MXU_W = 256
VMEM_LIMIT = 56 * 1024 * 1024
N_BISECT = 20

BF16 = jnp.bfloat16
F32 = jnp.float32


def _dot(a, b):
    return jnp.dot(a, b, preferred_element_type=F32)


def _rms(x, g):
    return x * lax.rsqrt(jnp.mean(x * x, axis=-1, keepdims=True) + EPS) * g


def _const_spec(shape):
    nd = len(shape)
    return pl.BlockSpec(shape, lambda *_: (0,) * nd, pipeline_mode=pl.Buffered(1))


def _layer_spec(a, layer):
    nd = a.ndim
    return pl.BlockSpec((None,) + a.shape[1:], lambda *_: (layer,) + (0,) * (nd - 1),
                        pipeline_mode=pl.Buffered(1))


def _ffn_kernel(*refs, d_ff, chunks, has_mix, has_final):
    it = iter(refs)
    x_ref = next(it)
    if has_mix:
        oatt_ref, gbyb_ref, mixg_ref, wga_ref, wb1_ref, wout_ref = (next(it) for _ in range(6))
    g_ref, wup_ref, wdn_ref = next(it), next(it), next(it)
    fin_ref = next(it) if has_final else None
    o_ref = next(it)

    x = x_ref[...]
    if has_mix:
        hm = _rms(x, mixg_ref[...]).astype(BF16)
        g_a = jax.nn.sigmoid(_dot(hm, wga_ref[...]))
        y_a = _dot(oatt_ref[...], wb1_ref[...])
        mixed = g_a * y_a + gbyb_ref[...].astype(F32)
        x = x + _dot(mixed.astype(BF16), wout_ref[...])
    h = _rms(x, g_ref[...]).astype(BF16)
    acc = jnp.zeros(x.shape, F32)
    for c0, c1 in chunks:
        a = _dot(h, wup_ref[:, c0:c1])
        b = _dot(h, wup_ref[:, d_ff + c0:d_ff + c1])
        act = (a * jax.nn.sigmoid(a) * b).astype(BF16)
        acc = acc + _dot(act, wdn_ref[c0:c1, :])
    y = x + 0.5 * acc
    if has_final:
        y = _rms(y, fin_ref[...])
    o_ref[...] = y


def _ffn_call(x, w, layer, which, *, mix=None, final_g=None, tm):
    n, d = x.shape
    g, w_up, w_dn = w[which + "_g"], w[which + "_up"], w[which + "_dn"]
    d_ff = w_dn.shape[1]
    tiles = d_ff // MXU_W
    split = (tiles + 1) // 2 * MXU_W
    chunks = ((0, split), (split, d_ff)) if d_ff % MXU_W == 0 and tiles >= 2 else ((0, d_ff),)
    row = lambda width: pl.BlockSpec((tm, width), lambda i: (i, 0))
    args, specs = [x], [row(d)]
    if mix is not None:
        oatt, gbyb = mix
        consts = [w["mix_g"], w["wga"], w["wb1"], w["wout"]]
        args += [oatt, gbyb] + consts
        specs += [row(oatt.shape[1]), row(d)] + [_layer_spec(c, layer) for c in consts]
    args += [g, w_up, w_dn]
    specs += [_layer_spec(c, layer) for c in (g, w_up, w_dn)]
    if final_g is not None:
        args.append(final_g)
        specs.append(_const_spec(final_g.shape))
    kern = functools.partial(_ffn_kernel, d_ff=d_ff, chunks=chunks,
                             has_mix=mix is not None, has_final=final_g is not None)
    return pl.pallas_call(
        kern,
        out_shape=jax.ShapeDtypeStruct((n, d), F32),
        grid=(n // tm,),
        in_specs=specs,
        out_specs=row(d),
        compiler_params=pltpu.CompilerParams(dimension_semantics=("arbitrary",),
                                             vmem_limit_bytes=VMEM_LIMIT),
        name="ffn_mix" if mix is not None else "ffn",
    )(*args)


def _rope(y, c, s):
    n = y.shape[-1]
    lane = lax.broadcasted_iota(jnp.int32, y.shape, 1)
    first = (lane & (HEAD_DIM - 1)) < HEAD_DIM // 2
    rot = jnp.where(first, pltpu.roll(y, n - HEAD_DIM // 2, 1), pltpu.roll(y, HEAD_DIM // 2, 1))
    return y * c + rot * s


def _proj_kernel(x_ref, g_ref, wqkv_ref, wqi_ref, wkw_ref, wconv_ref, wgb_ref, wb2_ref,
                 convw_ref, cprev_ref, cos_ref, sin_ref, coskw_ref, sinkw_ref,
                 kf_ref, vf_ref, kif_ref, kb_ref, kib_ref, q_ref, v_ref, qi_ref, wi_ref,
                 gbyb_ref, nconv_ref, ubuf, *, tm, transposed):
    t = pl.program_id(1)
    h = _rms(x_ref[0], g_ref[...]).astype(BF16)
    cosf, sinf = cos_ref[...], sin_ref[...]

    q = _rope(_dot(h, wqkv_ref[:, 0:ATT_W]), cosf, sinf) * (HEAD_DIM ** -0.5)
    k = _rope(_dot(h, wqkv_ref[:, ATT_W:2 * ATT_W]), cosf, sinf)
    v = _dot(h, wqkv_ref[:, 2 * ATT_W:3 * ATT_W])
    qi = _rope(_dot(h, wqi_ref[...]), cosf[:, :IDX_W], sinf[:, :IDX_W])
    kw = _rope(_dot(h, wkw_ref[...]), coskw_ref[...], sinkw_ref[...])

    kf_ref[0] = k
    vf_ref[0] = v
    kif_ref[0] = kw[:, :IDX_DIM]
    kb_ref[0, 0] = k.astype(BF16)
    kib_ref[0, 0] = kw[:, :IDX_DIM].astype(BF16)
    if transposed:
        q_ref[0] = q.T.astype(BF16)
        v_ref[0, 0] = v.T.astype(BF16)
        qi_ref[0] = qi.T.astype(BF16)
        wi_ref[0] = kw.T[IDX_DIM:IDX_DIM + SUBLANES, :]
    else:
        q_ref[0] = q.astype(BF16)
        v_ref[0] = v.astype(BF16)
        qi_ref[0] = qi.astype(BF16)
        wi_ref[0] = kw

    cb = _dot(h, wconv_ref[:, 0:CONV_W])
    cc = _dot(h, wconv_ref[:, CONV_W:2 * CONV_W])
    cx = _dot(h, wconv_ref[:, 2 * CONV_W:3 * CONV_W])
    u = cc * cx

    @pl.when(t == 0)
    def _():
        ubuf[SUBLANES - 2:SUBLANES, :] = cprev_ref[0]

    ubuf[SUBLANES:SUBLANES + tm, :] = u
    um1 = ubuf[SUBLANES - 1:SUBLANES - 1 + tm, :]
    um2 = ubuf[SUBLANES - 2:SUBLANES - 2 + tm, :]
    cw = convw_ref[...]
    y_conv = cw[0:1, :] * um2 + cw[1:2, :] * um1 + cw[2:3, :] * u
    tail = u[tm - 2:tm, :]
    ubuf[SUBLANES - 2:SUBLANES, :] = tail
    nconv_ref[0] = tail

    y_b = _dot((cb * y_conv).astype(BF16), wb2_ref[...])
    g_b = jax.nn.sigmoid(_dot(h, wgb_ref[...]))
    gbyb_ref[0] = (g_b * y_b).astype(BF16)


def _proj_call(x, w, layer, conv_prev, tabs, *, tm, transposed):
    b, t, d = x.shape
    nt = t // tm
    cos, sin, coskw, sinkw = tabs
    consts = [w["mix_g"], w["wqkv"], w["wqi"], w["wkw"], w["wconv"], w["wgb"], w["wb2"], w["conv_w"]]
    tile3 = lambda width: pl.BlockSpec((1, tm, width), lambda i, j: (i, j, 0))
    tile4 = lambda width: pl.BlockSpec((1, 1, tm, width), lambda i, j: (i, j, 0, 0))
    tab = lambda a: pl.BlockSpec((tm, a.shape[1]), lambda i, j: (j, 0))
    in_specs = ([tile3(d)] + [_layer_spec(c, layer) for c in consts]
                + [pl.BlockSpec((1, CONV_K - 1, CONV_W), lambda i, j: (i, 0, 0))]
                + [tab(cos), tab(sin), tab(coskw), tab(sinkw)])
    out_shape = [
        jax.ShapeDtypeStruct((b, t, ATT_W), F32),
        jax.ShapeDtypeStruct((b, t, ATT_W), F32),
        jax.ShapeDtypeStruct((b, t, IDX_DIM), F32),
        jax.ShapeDtypeStruct((b, nt, tm, ATT_W), BF16),
        jax.ShapeDtypeStruct((b, nt, tm, IDX_DIM), BF16),
    ]
    out_specs = [tile3(ATT_W), tile3(ATT_W), tile3(IDX_DIM), tile4(ATT_W), tile4(IDX_DIM)]
    if transposed:
        out_shape += [
            jax.ShapeDtypeStruct((b, ATT_W, t), BF16),
            jax.ShapeDtypeStruct((b, nt, ATT_W, tm), BF16),
            jax.ShapeDtypeStruct((b, IDX_W, t), BF16),
            jax.ShapeDtypeStruct((b, SUBLANES, t), F32),
        ]
        out_specs += [
            pl.BlockSpec((1, ATT_W, tm), lambda i, j: (i, 0, j)),
            pl.BlockSpec((1, 1, ATT_W, tm), lambda i, j: (i, j, 0, 0)),
            pl.BlockSpec((1, IDX_W, tm), lambda i, j: (i, 0, j)),
            pl.BlockSpec((1, SUBLANES, tm), lambda i, j: (i, 0, j)),
        ]
    else:
        out_shape += [
            jax.ShapeDtypeStruct((b, t, ATT_W), BF16),
            jax.ShapeDtypeStruct((b, t, ATT_W), BF16),
            jax.ShapeDtypeStruct((b, t, IDX_W), BF16),
            jax.ShapeDtypeStruct((b, t, LANES), F32),
        ]
        out_specs += [tile3(ATT_W), tile3(ATT_W), tile3(IDX_W), tile3(LANES)]
    out_shape += [
        jax.ShapeDtypeStruct((b, t, d), BF16),
        jax.ShapeDtypeStruct((b, CONV_K - 1, CONV_W), F32),
    ]
    out_specs += [tile3(d), pl.BlockSpec((1, CONV_K - 1, CONV_W), lambda i, j: (i, 0, 0))]
    kern = functools.partial(_proj_kernel, tm=tm, transposed=transposed)
    return pl.pallas_call(
        kern,
        out_shape=out_shape,
        grid=(b, nt),
        in_specs=in_specs,
        out_specs=out_specs,
        scratch_shapes=[pltpu.VMEM((tm + SUBLANES, CONV_W), F32)],
        compiler_params=pltpu.CompilerParams(dimension_semantics=("arbitrary", "arbitrary"),
                                             vmem_limit_bytes=VMEM_LIMIT),
        name="proj",
    )(x, *consts, conv_prev, cos, sin, coskw, sinkw)


def _slab_reduce(x, op):
    kb, qw = x.shape
    return op(x.reshape(kb // SUBLANES, SUBLANES, qw), axis=0)


def _attend_kernel(*refs, kb_rows, qw, n_kb_static, q_pos0, n_valid, top_k, assemble):
    j = pl.program_id(1)
    n_kb = (j + 1) if n_kb_static is None else n_kb_static
    kf = float(top_k)

    if assemble:
        (qi_ref, wi_ref, kic_ref, kin_ref, kc_ref, kn_ref, q_ref, vtc_ref, vtn_ref,
         o_ref, s_ref, lg_ref, ot_ref, ki_s, k_s, v_s) = refs
        past, new = kc_ref.shape[1], kn_ref.shape[0]
        for dst, cached, fresh in ((ki_s, kic_ref[...], kin_ref), (k_s, kc_ref[...].T.astype(BF16), kn_ref)):
            dst[0, 0:past, :] = cached
            dst[0, past:past + new, :] = fresh[...]
            dst[0, past + new:, :] = jnp.zeros((kb_rows - past - new, dst.shape[2]), BF16)
        v_s[0, :, 0:past] = vtc_ref[...].astype(BF16)
        v_s[0, :, past:] = vtn_ref[...]
        ki_blk = lambda kb: ki_s[kb]
        k_blk = lambda kb, lanes: k_s[kb, :, lanes]
        v_blk = lambda kb, rows: v_s[kb, rows, :]
    else:
        qi_ref, wi_ref, ki_ref, k_ref, q_ref, v_ref, o_ref, s_ref, lg_ref, ot_ref = refs
        ki_blk = lambda kb: ki_ref[0, kb]
        k_blk = lambda kb, lanes: k_ref[0, kb, :, lanes]
        v_blk = lambda kb, rows: v_ref[0, kb, rows, :]

    def over_blocks(body, init, n=None, scores_only=False):
        n = n_kb if n is None else n
        pair = lambda i, c: body([2 * i, 2 * i + 1], c)
        if isinstance(n, int):
            pairs = lax.fori_loop(0, n // 2, pair, init)
            return body([n - 1], pairs) if n % 2 else pairs
        if scores_only:
            return lax.fori_loop(0, (n + 1) // 2, pair, init)
        pairs = lax.fori_loop(0, n // 2, pair, init)
        return lax.cond(n % 2 == 1, lambda c: body([n - 1], c), lambda c: c, pairs)

    last = n_kb - 1
    row = lax.broadcasted_iota(jnp.int32, (kb_rows, qw), 0) + last * kb_rows
    qpos = lax.broadcasted_iota(jnp.int32, (kb_rows, qw), 1) + (q_pos0 + j * qw)
    chunk_shift = CHUNK.bit_length() - 1
    adm_last = ((row >> chunk_shift) <= (qpos >> chunk_shift)) & (row < n_valid)

    def score_block(kb):
        ki = ki_blk(kb)
        s = jnp.zeros((kb_rows, qw), F32)
        for hh in range(IDX_HEADS):
            d = _dot(ki, qi_ref[0, hh * IDX_DIM:(hh + 1) * IDX_DIM, :])
            s = s + jnp.maximum(d, 0.0) * wi_ref[0, hh:hh + 1, :]
        return s * IDX_SCALE

    def score_body(kbs, carry):
        mx, mn = carry
        for kb in kbs:
            s = score_block(kb)
            s_ref[kb] = s
            mx = jnp.maximum(mx, _slab_reduce(s, jnp.max))
            mn = jnp.minimum(mn, _slab_reduce(s, jnp.min))
        return mx, mn

    mx8 = jnp.full((SUBLANES, qw), -BIG, F32)
    mn8 = jnp.full((SUBLANES, qw), BIG, F32)
    mx8, mn8 = over_blocks(score_body, (mx8, mn8), n=last)
    s_last = score_block(last)
    s_ref[last] = jnp.where(adm_last, s_last, NEG)
    s_ref[n_kb] = jnp.full((kb_rows, qw), NEG, F32)
    mx8 = jnp.maximum(mx8, _slab_reduce(jnp.where(adm_last, s_last, -BIG), jnp.max))
    mn8 = jnp.minimum(mn8, _slab_reduce(jnp.where(adm_last, s_last, BIG), jnp.min))
    smax = jnp.max(mx8, axis=0, keepdims=True)
    smin = jnp.min(mn8, axis=0, keepdims=True)
    n_adm = (jnp.sum(_slab_reduce(jnp.where(adm_last, 1.0, 0.0), jnp.sum), axis=0, keepdims=True)
             + jnp.float32(1.0) * (last * kb_rows))

    def count(pred_fn):
        def body(kbs, acc):
            for kb in kbs:
                acc = acc + _slab_reduce(jnp.where(pred_fn(s_ref[kb]), 1.0, 0.0), jnp.sum)
            return acc
        acc8 = over_blocks(body, jnp.zeros((SUBLANES, qw), F32), scores_only=True)
        return jnp.sum(acc8, axis=0, keepdims=True)

    def bisect(_, carry):
        lo, hi, c_hi = carry
        mid = 0.5 * lo + 0.5 * hi
        cnt = count(lambda s: s >= mid)
        ge = cnt >= kf
        return jnp.where(ge, mid, lo), jnp.where(ge, hi, mid), jnp.where(ge, c_hi, cnt)

    hi0 = smax + (jnp.abs(smax) + 1.0) * (2.0 ** -10)
    _, hi, c_hi = lax.fori_loop(0, N_BISECT, bisect, (smin, hi0, jnp.zeros((1, qw), F32)))

    def snap(carry):
        hi, c_hi, done, tau, n_gt = carry

        def vbody(kbs, acc):
            for kb in kbs:
                s = s_ref[kb]
                acc = jnp.maximum(acc, _slab_reduce(jnp.where(s < hi, s, -BIG), jnp.max))
            return acc
        cand8 = over_blocks(vbody, jnp.full((SUBLANES, qw), -BIG, F32), scores_only=True)
        cand = jnp.max(cand8, axis=0, keepdims=True)
        c_ge = count(lambda s: s >= cand)
        ok = c_ge >= kf
        fresh = ok & (done < 0.5)
        return (jnp.where(ok, hi, cand), jnp.where(ok, c_hi, c_ge), jnp.where(ok, 1.0, done),
                jnp.where(fresh, cand, tau), jnp.where(fresh, c_hi, n_gt))

    few = n_adm <= kf
    carry0 = (hi, c_hi, jnp.where(few, 1.0, 0.0), jnp.full((1, qw), 0.5 * NEG, F32), jnp.zeros((1, qw), F32))
    _, _, _, tau, n_gt = lax.while_loop(lambda c: jnp.min(c[2]) < 0.5, snap, carry0)
    need = kf - n_gt

    r_i = lax.broadcasted_iota(jnp.int32, (kb_rows, kb_rows), 0)
    c_i = lax.broadcasted_iota(jnp.int32, (kb_rows, kb_rows), 1)
    tri = jnp.where(c_i <= r_i, 1.0, 0.0).astype(BF16)

    def select_body(kbs, off):
        for kb in kbs:
            s = s_ref[kb]
            eq = s == tau
            rank = _dot(tri, jnp.where(eq, 1.0, 0.0).astype(BF16))
            y = jnp.where(eq, rank + off, jnp.where(s > tau, 0.0, BIG))
            s_ref[kb] = jnp.where(y <= need, 0.0, NEG)
            off = off + rank[kb_rows - 1:kb_rows, :]
        return off

    over_blocks(select_body, jnp.zeros((1, qw), F32))

    pair_w = 2 * HEAD_DIM
    pr = lax.broadcasted_iota(jnp.int32, (pair_w, qw), 0)
    q_pairs = []
    for pair in range(N_HEADS // 2):
        qp = q_ref[0, pair * pair_w:(pair + 1) * pair_w, :]
        zero = jnp.zeros_like(qp)
        q_pairs.append(jnp.concatenate([jnp.where(pr < HEAD_DIM, qp, zero),
                                        jnp.where(pr >= HEAD_DIM, qp, zero)], axis=1))

    ot_ref[...] = jnp.zeros(ot_ref.shape, F32)
    head_rows = [slice(hd * HEAD_DIM, (hd + 1) * HEAD_DIM) for hd in range(N_HEADS)]

    def logit_body(kbs, m8s):
        m8s = list(m8s)
        for kb in kbs:
            bias = s_ref[kb]
            for pair in range(N_HEADS // 2):
                lg2 = _dot(k_blk(kb, slice(pair * pair_w, (pair + 1) * pair_w)), q_pairs[pair])
                for half in range(2):
                    hd = 2 * pair + half
                    lg = lg2[:, half * qw:(half + 1) * qw] + bias
                    lg_ref[hd, kb] = lg
                    m8s[hd] = jnp.maximum(m8s[hd], _slab_reduce(lg, jnp.max))
        return tuple(m8s)

    m8s = over_blocks(logit_body, tuple(jnp.full((SUBLANES, qw), -BIG, F32) for _ in range(N_HEADS)))
    ms = [jnp.max(m8, axis=0, keepdims=True) for m8 in m8s]

    def pv_body(kbs, l8s):
        l8s = list(l8s)
        for hd in range(N_HEADS):
            rows = head_rows[hd]
            acc = ot_ref[rows, :]
            for kb in kbs:
                e = jnp.exp(lg_ref[hd, kb] - ms[hd])
                acc = acc + _dot(v_blk(kb, rows), e.astype(BF16))
                l8s[hd] = l8s[hd] + _slab_reduce(e, jnp.sum)
            ot_ref[rows, :] = acc
        return tuple(l8s)

    l8s = over_blocks(pv_body, tuple(jnp.zeros((SUBLANES, qw), F32) for _ in range(N_HEADS)))
    for hd in range(N_HEADS):
        rows = head_rows[hd]
        ot_ref[rows, :] = ot_ref[rows, :] / jnp.sum(l8s[hd], axis=0, keepdims=True)

    o_ref[0] = ot_ref[...].T.astype(BF16)


def _attend_call(qi_t, wi_t, q_t, keys, *, layer=None, qw, kb_rows, n_kb_max, n_kb_static,
                 q_pos0, n_valid, top_k):
    b, _, t_q = q_t.shape
    n_q = t_q // qw
    assemble = layer is not None
    qspec = lambda rows: pl.BlockSpec((1, rows, qw), lambda i, j: (i, 0, j))
    scratch = [pltpu.VMEM((n_kb_max + 1, kb_rows, qw), F32), pltpu.VMEM((N_HEADS, n_kb_max, kb_rows, qw), F32),
               pltpu.VMEM((ATT_W, qw), F32)]
    if assemble:
        kic, kc, vtc, kin, kn, vtn = keys
        cache = lambda a: pl.BlockSpec((None, None) + a.shape[2:], lambda i, j: (layer, i, 0, 0))
        fresh = lambda a: pl.BlockSpec((None,) + a.shape[1:], lambda i, j: (i, 0, 0))
        args = [qi_t, wi_t, kic, kin, kc, kn, q_t, vtc, vtn]
        in_specs = [qspec(IDX_W), qspec(SUBLANES), cache(kic), fresh(kin), cache(kc), fresh(kn),
                    qspec(ATT_W), cache(vtc), fresh(vtn)]
        scratch += [pltpu.VMEM((1, kb_rows, IDX_DIM), BF16), pltpu.VMEM((1, kb_rows, ATT_W), BF16),
                    pltpu.VMEM((1, ATT_W, kb_rows), BF16)]
    else:
        ki_b, k_b, v_tb = keys
        batch_blk = lambda a: pl.BlockSpec((1,) + a.shape[1:], lambda i, j: (i, 0, 0, 0))
        args = [qi_t, wi_t, ki_b, k_b, q_t, v_tb]
        in_specs = [qspec(IDX_W), qspec(SUBLANES), batch_blk(ki_b), batch_blk(k_b), qspec(ATT_W),
                    batch_blk(v_tb)]
    kern = functools.partial(_attend_kernel, kb_rows=kb_rows, qw=qw, n_kb_static=n_kb_static,
                             q_pos0=q_pos0, n_valid=n_valid, top_k=top_k, assemble=assemble)
    return pl.pallas_call(
        kern,
        out_shape=jax.ShapeDtypeStruct((b, t_q, ATT_W), BF16),
        grid=(b, n_q),
        in_specs=in_specs,
        out_specs=pl.BlockSpec((1, qw, ATT_W), lambda i, j: (i, j, 0)),
        scratch_shapes=scratch,
        compiler_params=pltpu.CompilerParams(dimension_semantics=("arbitrary", "arbitrary"),
                                             vmem_limit_bytes=VMEM_LIMIT),
        name="attend_cached" if assemble else "attend",
    )(*args)


def _rope_tables(pos):
    half = HEAD_DIM // 2
    freqs = ROPE_THETA ** (-jnp.arange(half, dtype=F32) / half)
    ang = pos.astype(F32)[:, None] * freqs[None, :]
    cos, sin = jnp.cos(ang), jnp.sin(ang)
    cos_h = jnp.concatenate([cos, cos], axis=-1)
    sin_h = jnp.concatenate([-sin, sin], axis=-1)
    n = pos.shape[0]
    pad_c = jnp.ones((n, LANES - IDX_DIM), F32)
    pad_s = jnp.zeros((n, LANES - IDX_DIM), F32)
    return (jnp.tile(cos_h, (1, N_HEADS)), jnp.tile(sin_h, (1, N_HEADS)),
            jnp.concatenate([cos_h, pad_c], axis=-1), jnp.concatenate([sin_h, pad_s], axis=-1))


def _prep_weights(ffn1_norm, ffn1_up, ffn1_down, mix_norm, w_in, conv_w, w_branch, w_out,
                  ffn2_norm, ffn2_up, ffn2_down):
    d = w_in.shape[1]
    c = lambda a: a.astype(BF16)
    gain = lambda a: a[:, None, :]
    sizes = (3 * ATT_W, IDX_W, IDX_DIM + IDX_HEADS, 3 * CONV_W, d, d)
    offs = [0]
    for s in sizes:
        offs.append(offs[-1] + s)
    wqkv, wqi, wkw, wconv, wga, wgb = (w_in[:, :, offs[n]:offs[n + 1]] for n in range(len(sizes)))
    wkw = jnp.pad(wkw, ((0, 0), (0, 0), (0, LANES - IDX_DIM - IDX_HEADS)))
    return dict(
        ffn1_g=gain(ffn1_norm), ffn1_up=c(ffn1_up), ffn1_dn=c(ffn1_down),
        mix_g=gain(mix_norm), wqkv=c(wqkv), wqi=c(wqi), wkw=c(wkw), wconv=c(wconv),
        wga=c(wga), wgb=c(wgb), conv_w=conv_w,
        wb1=c(w_branch[:, :ATT_W]), wb2=c(w_branch[:, ATT_W:]), wout=c(w_out),
        ffn2_g=gain(ffn2_norm), ffn2_up=c(ffn2_up), ffn2_dn=c(ffn2_down),
    )


def _pick_tile(n, pref):
    t = min(pref, n)
    while n % t:
        t //= 2
    return t


def kernel(x_prompt, x_sample, cache_k, cache_v, cache_kidx, state_conv, ffn1_norm, ffn1_up, ffn1_down, mix_norm, w_in, conv_w, w_branch, w_out, ffn2_norm, ffn2_up, ffn2_down, final_norm):
    depth = w_in.shape[0]
    bp, tp, d = x_prompt.shape
    bs, ts, _ = x_sample.shape
    past = cache_k.shape[2]
    assert past % LANES == 0 and ts % (2 * SUBLANES) == 0
    qw_p = 2 * LANES
    qw_s = LANES
    l_s = past + ts
    kb_s = -(-l_s // LANES) * LANES
    topk_p = min(TOPK_MAX, tp // 4)
    topk_s = min(TOPK_MAX, l_s // 4)

    w = _prep_weights(ffn1_norm, ffn1_up, ffn1_down, mix_norm, w_in, conv_w, w_branch, w_out,
                      ffn2_norm, ffn2_up, ffn2_down)
    tabs_p = _rope_tables(jnp.arange(tp))
    tabs_s = _rope_tables(past + jnp.arange(ts))
    fin = final_norm[None]
    zero_conv = jnp.zeros((bp, CONV_K - 1, CONV_W), F32)
    tm_p = _pick_tile(bp * tp, 512)
    tm_s = _pick_tile(bs * ts, 512)

    kc_all = jnp.swapaxes(cache_k.reshape(depth, bs, past, ATT_W), 2, 3)
    vtc_all = jnp.swapaxes(cache_v.reshape(depth, bs, past, ATT_W), 2, 3)
    kic_all = cache_kidx.astype(BF16)

    xp = x_prompt.reshape(bp * tp, d)
    xs = x_sample.reshape(bs * ts, d)
    outs = [[] for _ in range(8)]
    for i in range(depth):
        last = i == depth - 1

        xp = _ffn_call(xp, w, i, "ffn1", tm=tm_p)
        (kf, vf, kif, kb, kib, qt, vtb, qit, wit, gbyb, nconv) = _proj_call(
            xp.reshape(bp, tp, d), w, i, zero_conv, tabs_p, tm=qw_p, transposed=True)
        oatt = _attend_call(qit, wit, qt, (kib, kb, vtb), qw=qw_p, kb_rows=qw_p, n_kb_max=tp // qw_p,
                            n_kb_static=None, q_pos0=0, n_valid=tp, top_k=topk_p)
        xp = _ffn_call(xp, w, i, "ffn2", mix=(oatt.reshape(bp * tp, ATT_W), gbyb.reshape(bp * tp, d)),
                       final_g=fin if last else None, tm=tm_p)
        for lst, a in zip(outs[:4], (kf, vf, kif, nconv)):
            lst.append(a)

        xs = _ffn_call(xs, w, i, "ffn1", tm=tm_s)
        (kf, vf, kif, kb, kib, qn, vn, qin, win, gbyb, nconv) = _proj_call(
            xs.reshape(bs, ts, d), w, i, state_conv[i], tabs_s, tm=ts, transposed=False)
        lane_pad = lambda a, width: jnp.pad(jnp.swapaxes(a, 1, 2), ((0, 0), (0, 0), (0, width - ts)))
        oatt = _attend_call(lane_pad(qin, qw_s), lane_pad(win[:, :, IDX_DIM:IDX_DIM + SUBLANES], qw_s),
                            lane_pad(qn, qw_s),
                            (kic_all, kc_all, vtc_all, kib[:, 0], kb[:, 0], lane_pad(vn, kb_s - past)),
                            layer=i, qw=qw_s, kb_rows=kb_s, n_kb_max=1, n_kb_static=1,
                            q_pos0=past, n_valid=l_s, top_k=topk_s)[:, :ts]
        xs = _ffn_call(xs, w, i, "ffn2", mix=(oatt.reshape(bs * ts, ATT_W), gbyb.reshape(bs * ts, d)),
                       final_g=fin if last else None, tm=tm_s)
        for lst, a in zip(outs[4:], (kf, vf, kif, nconv)):
            lst.append(a)

    heads = lambda a: a.reshape(a.shape[:-1] + (N_HEADS, HEAD_DIM))
    st = [jnp.stack(l) for l in outs]
    return (xp.reshape(bp, tp, d), xs.reshape(bs, ts, d),
            heads(st[0]), heads(st[1]), st[2], st[3],
            heads(st[4]), heads(st[5]), st[6], st[7])
```

```python
import functools
import math

import jax
import jax.numpy as jnp
from jax import lax
from jax.experimental import pallas as pl
from jax.experimental.pallas import tpu as pltpu

CHUNK = 64
N_HEADS = 8
HEAD_DIM = 64
ATT_W = N_HEADS * HEAD_DIM
IDX_HEADS = 4
IDX_DIM = 64
IDX_W = IDX_HEADS * IDX_DIM
TOPK_MAX = 256
CONV_W = 512
CONV_K = 3
ROPE_THETA = 10000.0
EPS = 1e-6
NEG = -1e30
BIG = 3e38
IDX_SCALE = (IDX_DIM ** -0.5) * (IDX_HEADS ** -0.5)
Q_SCALE = (HEAD_DIM ** -0.5) * math.log2(math.e)

LANES = 128
SUBLANES = 8
MXU_W = 256
VMEM_LIMIT = 56 * 1024 * 1024
N_BISECT = 20

BF16 = jnp.bfloat16
F32 = jnp.float32


def _dot(a, b):
    return jnp.dot(a, b, preferred_element_type=F32)


def _rms(x, g):
    return x * lax.rsqrt(jnp.mean(x * x, axis=-1, keepdims=True) + EPS) * g


def _const_spec(shape):
    nd = len(shape)
    return pl.BlockSpec(shape, lambda *_: (0,) * nd, pipeline_mode=pl.Buffered(1))


def _layer_spec(a, layer):
    nd = a.ndim
    return pl.BlockSpec((None,) + a.shape[1:], lambda *_: (layer,) + (0,) * (nd - 1),
                        pipeline_mode=pl.Buffered(1))


def _ffn_kernel(*refs, d_ff, chunks, has_mix, has_final):
    it = iter(refs)
    x_ref = next(it)
    if has_mix:
        oatt_ref, gbyb_ref, mixg_ref, wga_ref, wb1_ref, wout_ref = (next(it) for _ in range(6))
    g_ref, wup_ref, wdn_ref = next(it), next(it), next(it)
    fin_ref = next(it) if has_final else None
    o_ref = next(it)

    x = x_ref[...]
    if has_mix:
        hm = _rms(x, mixg_ref[...]).astype(BF16)
        g_a = jax.nn.sigmoid(_dot(hm, wga_ref[...]))
        y_a = _dot(oatt_ref[...], wb1_ref[...])
        mixed = g_a * y_a + gbyb_ref[...].astype(F32)
        x = x + _dot(mixed.astype(BF16), wout_ref[...])
    h = _rms(x, g_ref[...]).astype(BF16)
    acc = jnp.zeros(x.shape, F32)
    for c0, c1 in chunks:
        a = _dot(h, wup_ref[:, c0:c1])
        b = _dot(h, wup_ref[:, d_ff + c0:d_ff + c1])
        act = (a * jax.nn.sigmoid(a) * b).astype(BF16)
        acc = acc + _dot(act, wdn_ref[c0:c1, :])
    y = x + 0.5 * acc
    if has_final:
        y = _rms(y, fin_ref[...])
    o_ref[...] = y


def _ffn_call(x, w, layer, which, *, mix=None, final_g=None, tm):
    n, d = x.shape
    g, w_up, w_dn = w[which + "_g"], w[which + "_up"], w[which + "_dn"]
    d_ff = w_dn.shape[1]
    tiles = d_ff // MXU_W
    split = (tiles + 1) // 2 * MXU_W
    chunks = ((0, split), (split, d_ff)) if d_ff % MXU_W == 0 and tiles >= 2 else ((0, d_ff),)
    row = lambda width: pl.BlockSpec((tm, width), lambda i: (i, 0))
    args, specs = [x], [row(d)]
    if mix is not None:
        oatt, gbyb = mix
        consts = [w["mix_g"], w["wga"], w["wb1"], w["wout"]]
        args += [oatt, gbyb] + consts
        specs += [row(oatt.shape[1]), row(d)] + [_layer_spec(c, layer) for c in consts]
    args += [g, w_up, w_dn]
    specs += [_layer_spec(c, layer) for c in (g, w_up, w_dn)]
    if final_g is not None:
        args.append(final_g)
        specs.append(_const_spec(final_g.shape))
    kern = functools.partial(_ffn_kernel, d_ff=d_ff, chunks=chunks,
                             has_mix=mix is not None, has_final=final_g is not None)
    return pl.pallas_call(
        kern,
        out_shape=jax.ShapeDtypeStruct((n, d), F32),
        grid=(n // tm,),
        in_specs=specs,
        out_specs=row(d),
        compiler_params=pltpu.CompilerParams(dimension_semantics=("arbitrary",),
                                             vmem_limit_bytes=VMEM_LIMIT),
        name="ffn_mix" if mix is not None else "ffn",
    )(*args)


def _rope(y, c, s):
    n = y.shape[-1]
    lane = lax.broadcasted_iota(jnp.int32, y.shape, 1)
    first = (lane & (HEAD_DIM - 1)) < HEAD_DIM // 2
    rot = jnp.where(first, pltpu.roll(y, n - HEAD_DIM // 2, 1), pltpu.roll(y, HEAD_DIM // 2, 1))
    return y * c + rot * s


def _proj_kernel(x_ref, g_ref, wqkv_ref, wqi_ref, wkw_ref, wconv_ref, wgb_ref, wb2_ref,
                 convw_ref, cprev_ref, cos_ref, sin_ref, coskw_ref, sinkw_ref,
                 kf_ref, vf_ref, kif_ref, kb_ref, kib_ref, q_ref, v_ref, qi_ref, wi_ref,
                 gbyb_ref, nconv_ref, ubuf, *, tm, kb_rows, transposed):
    t = pl.program_id(1)
    h = _rms(x_ref[0], g_ref[...]).astype(BF16)
    cosf, sinf = cos_ref[...], sin_ref[...]

    q = _rope(_dot(h, wqkv_ref[:, 0:ATT_W]), cosf, sinf) * Q_SCALE
    k = _rope(_dot(h, wqkv_ref[:, ATT_W:2 * ATT_W]), cosf, sinf)
    v = _dot(h, wqkv_ref[:, 2 * ATT_W:3 * ATT_W])
    qi = _rope(_dot(h, wqi_ref[...]), cosf[:, :IDX_W], sinf[:, :IDX_W])
    kw = _rope(_dot(h, wkw_ref[...]), coskw_ref[...], sinkw_ref[...])

    kf_ref[0] = k
    vf_ref[0] = v
    kif_ref[0] = kw[:, :IDX_DIM]
    n_blk = tm // kb_rows
    kb_ref[0] = k.astype(BF16).reshape(n_blk, kb_rows, ATT_W)
    kib_ref[0] = kw[:, :IDX_DIM].astype(BF16).reshape(n_blk, kb_rows, IDX_DIM)
    if transposed:
        q_ref[0] = q.T.astype(BF16)
        vt = v.T.astype(BF16)
        for n in range(n_blk):
            v_ref[0, n] = vt[:, n * kb_rows:(n + 1) * kb_rows]
        qi_ref[0] = qi.T.astype(BF16)
        wi_ref[0] = kw.T[IDX_DIM:IDX_DIM + SUBLANES, :]
    else:
        q_ref[0] = q.astype(BF16)
        v_ref[0] = v.astype(BF16)
        qi_ref[0] = qi.astype(BF16)
        wi_ref[0] = kw

    cb = _dot(h, wconv_ref[:, 0:CONV_W])
    cc = _dot(h, wconv_ref[:, CONV_W:2 * CONV_W])
    cx = _dot(h, wconv_ref[:, 2 * CONV_W:3 * CONV_W])
    u = cc * cx

    @pl.when(t == 0)
    def _():
        ubuf[SUBLANES - 2:SUBLANES, :] = cprev_ref[0]

    ubuf[SUBLANES:SUBLANES + tm, :] = u
    um1 = ubuf[SUBLANES - 1:SUBLANES - 1 + tm, :]
    um2 = ubuf[SUBLANES - 2:SUBLANES - 2 + tm, :]
    cw = convw_ref[...]
    y_conv = cw[0:1, :] * um2 + cw[1:2, :] * um1 + cw[2:3, :] * u
    tail = u[tm - 2:tm, :]
    ubuf[SUBLANES - 2:SUBLANES, :] = tail
    nconv_ref[0] = tail

    y_b = _dot((cb * y_conv).astype(BF16), wb2_ref[...])
    g_b = jax.nn.sigmoid(_dot(h, wgb_ref[...]))
    gbyb_ref[0] = (g_b * y_b).astype(BF16)


def _proj_call(x, w, layer, conv_prev, tabs, *, tm, kb_rows, transposed):
    b, t, d = x.shape
    nt = t // tm
    cos, sin, coskw, sinkw = tabs
    consts = [w["mix_g"], w["wqkv"], w["wqi"], w["wkw"], w["wconv"], w["wgb"], w["wb2"], w["conv_w"]]
    tile3 = lambda width: pl.BlockSpec((1, tm, width), lambda i, j: (i, j, 0))
    n_blk = tm // kb_rows
    tile4 = lambda width: pl.BlockSpec((1, n_blk, kb_rows, width), lambda i, j: (i, j, 0, 0))
    tab = lambda a: pl.BlockSpec((tm, a.shape[1]), lambda i, j: (j, 0))
    in_specs = ([tile3(d)] + [_layer_spec(c, layer) for c in consts]
                + [pl.BlockSpec((1, CONV_K - 1, CONV_W), lambda i, j: (i, 0, 0))]
                + [tab(cos), tab(sin), tab(coskw), tab(sinkw)])
    out_shape = [
        jax.ShapeDtypeStruct((b, t, ATT_W), F32),
        jax.ShapeDtypeStruct((b, t, ATT_W), F32),
        jax.ShapeDtypeStruct((b, t, IDX_DIM), F32),
        jax.ShapeDtypeStruct((b, t // kb_rows, kb_rows, ATT_W), BF16),
        jax.ShapeDtypeStruct((b, t // kb_rows, kb_rows, IDX_DIM), BF16),
    ]
    out_specs = [tile3(ATT_W), tile3(ATT_W), tile3(IDX_DIM), tile4(ATT_W), tile4(IDX_DIM)]
    if transposed:
        out_shape += [
            jax.ShapeDtypeStruct((b, ATT_W, t), BF16),
            jax.ShapeDtypeStruct((b, t // kb_rows, ATT_W, kb_rows), BF16),
            jax.ShapeDtypeStruct((b, IDX_W, t), BF16),
            jax.ShapeDtypeStruct((b, SUBLANES, t), F32),
        ]
        out_specs += [
            pl.BlockSpec((1, ATT_W, tm), lambda i, j: (i, 0, j)),
            pl.BlockSpec((1, n_blk, ATT_W, kb_rows), lambda i, j: (i, j, 0, 0)),
            pl.BlockSpec((1, IDX_W, tm), lambda i, j: (i, 0, j)),
            pl.BlockSpec((1, SUBLANES, tm), lambda i, j: (i, 0, j)),
        ]
    else:
        out_shape += [
            jax.ShapeDtypeStruct((b, t, ATT_W), BF16),
            jax.ShapeDtypeStruct((b, t, ATT_W), BF16),
            jax.ShapeDtypeStruct((b, t, IDX_W), BF16),
            jax.ShapeDtypeStruct((b, t, LANES), F32),
        ]
        out_specs += [tile3(ATT_W), tile3(ATT_W), tile3(IDX_W), tile3(LANES)]
    out_shape += [
        jax.ShapeDtypeStruct((b, t, d), BF16),
        jax.ShapeDtypeStruct((b, CONV_K - 1, CONV_W), F32),
    ]
    out_specs += [tile3(d), pl.BlockSpec((1, CONV_K - 1, CONV_W), lambda i, j: (i, 0, 0))]
    kern = functools.partial(_proj_kernel, tm=tm, kb_rows=kb_rows, transposed=transposed)
    return pl.pallas_call(
        kern,
        out_shape=out_shape,
        grid=(b, nt),
        in_specs=in_specs,
        out_specs=out_specs,
        scratch_shapes=[pltpu.VMEM((tm + SUBLANES, CONV_W), F32)],
        compiler_params=pltpu.CompilerParams(dimension_semantics=("arbitrary", "arbitrary"),
                                             vmem_limit_bytes=VMEM_LIMIT),
        name="proj",
    )(x, *consts, conv_prev, cos, sin, coskw, sinkw)


def _slab_reduce(x, op):
    kb, qw = x.shape
    return op(x.reshape(kb // SUBLANES, SUBLANES, qw), axis=0)


def _attend_kernel(*refs, kb_rows, qw, n_kb_static, q_pos0, n_valid, top_k, assemble):
    j = pl.program_id(1)
    n_kb = (j + 1) if n_kb_static is None else n_kb_static
    kf = float(top_k)

    if assemble:
        (qi_ref, wi_ref, kic_ref, kin_ref, kc_ref, kn_ref, q_ref, vtc_ref, vtn_ref,
         o_ref, s_ref, lg_ref, ot_ref, ki_s, k_s, v_s) = refs
        past, new = kc_ref.shape[1], kn_ref.shape[0]
        for dst, cached, fresh in ((ki_s, kic_ref[...], kin_ref), (k_s, kc_ref[...].T.astype(BF16), kn_ref)):
            dst[0, 0:past, :] = cached
            dst[0, past:past + new, :] = fresh[...]
            dst[0, past + new:, :] = jnp.zeros((kb_rows - past - new, dst.shape[2]), BF16)
        v_s[0, :, 0:past] = vtc_ref[...].astype(BF16)
        v_s[0, :, past:] = vtn_ref[...]
        ki_blk = lambda kb: ki_s[kb]
        k_blk = lambda kb, lanes: k_s[kb, :, lanes]
        v_blk = lambda kb, rows: v_s[kb, rows, :]
    else:
        qi_ref, wi_ref, ki_ref, k_ref, q_ref, v_ref, o_ref, s_ref, lg_ref, ot_ref = refs
        ki_blk = lambda kb: ki_ref[0, kb]
        k_blk = lambda kb, lanes: k_ref[0, kb, :, lanes]
        v_blk = lambda kb, rows: v_ref[0, kb, rows, :]

    def over_blocks(body, init, n=None):
        n = n_kb if n is None else n
        pairs = lax.fori_loop(0, n // 2, lambda i, c: body([2 * i, 2 * i + 1], c), init)
        if isinstance(n, int):
            return body([n - 1], pairs) if n % 2 else pairs
        return lax.cond(n % 2 == 1, lambda c: body([n - 1], c), lambda c: c, pairs)

    last = n_kb - 1
    row = lax.broadcasted_iota(jnp.int32, (kb_rows, qw), 0) + last * kb_rows
    qpos = lax.broadcasted_iota(jnp.int32, (kb_rows, qw), 1) + (q_pos0 + j * qw)
    chunk_shift = CHUNK.bit_length() - 1
    adm_last = ((row >> chunk_shift) <= (qpos >> chunk_shift)) & (row < n_valid)

    def score_block(kb):
        ki = ki_blk(kb)
        s = jnp.zeros((kb_rows, qw), F32)
        for hh in range(IDX_HEADS):
            d = _dot(ki, qi_ref[0, hh * IDX_DIM:(hh + 1) * IDX_DIM, :])
            s = s + jnp.maximum(d, 0.0) * wi_ref[0, hh:hh + 1, :]
        return s * IDX_SCALE

    def score_body(kbs, carry):
        mx, mn = carry
        for kb in kbs:
            s = score_block(kb)
            s_ref[kb] = s
            mx = jnp.maximum(mx, _slab_reduce(s, jnp.max))
            mn = jnp.minimum(mn, _slab_reduce(s, jnp.min))
        return mx, mn

    mx8 = jnp.full((SUBLANES, qw), -BIG, F32)
    mn8 = jnp.full((SUBLANES, qw), BIG, F32)
    mx8, mn8 = over_blocks(score_body, (mx8, mn8), n=last)
    s_last = score_block(last)
    s_ref[last] = jnp.where(adm_last, s_last, NEG)
    mx8 = jnp.maximum(mx8, _slab_reduce(jnp.where(adm_last, s_last, -BIG), jnp.max))
    mn8 = jnp.minimum(mn8, _slab_reduce(jnp.where(adm_last, s_last, BIG), jnp.min))
    smax = jnp.max(mx8, axis=0, keepdims=True)
    smin = jnp.min(mn8, axis=0, keepdims=True)
    n_adm = (jnp.sum(_slab_reduce(jnp.where(adm_last, 1.0, 0.0), jnp.sum), axis=0, keepdims=True)
             + jnp.float32(1.0) * (last * kb_rows))

    def count(pred_fn):
        def body(kbs, acc):
            for kb in kbs:
                acc = acc + _slab_reduce(jnp.where(pred_fn(s_ref[kb]), 1.0, 0.0), jnp.sum)
            return acc
        return jnp.sum(over_blocks(body, jnp.zeros((SUBLANES, qw), F32)), axis=0, keepdims=True)

    def bisect(_, carry):
        lo, hi, c_hi = carry
        mid = 0.5 * lo + 0.5 * hi
        cnt = count(lambda s: s >= mid)
        ge = cnt >= kf
        return jnp.where(ge, mid, lo), jnp.where(ge, hi, mid), jnp.where(ge, c_hi, cnt)

    hi0 = smax + (jnp.abs(smax) + 1.0) * (2.0 ** -10)
    _, hi, c_hi = lax.fori_loop(0, N_BISECT, bisect, (smin, hi0, jnp.zeros((1, qw), F32)))

    def snap(carry):
        hi, c_hi, done, tau, n_gt = carry

        def vbody(kbs, acc):
            for kb in kbs:
                s = s_ref[kb]
                acc = jnp.maximum(acc, _slab_reduce(jnp.where(s < hi, s, -BIG), jnp.max))
            return acc
        cand = jnp.max(over_blocks(vbody, jnp.full((SUBLANES, qw), -BIG, F32)), axis=0, keepdims=True)
        c_ge = count(lambda s: s >= cand)
        ok = c_ge >= kf
        fresh = ok & (done < 0.5)
        return (jnp.where(ok, hi, cand), jnp.where(ok, c_hi, c_ge), jnp.where(ok, 1.0, done),
                jnp.where(fresh, cand, tau), jnp.where(fresh, c_hi, n_gt))

    few = n_adm <= kf
    carry0 = (hi, c_hi, jnp.where(few, 1.0, 0.0), jnp.full((1, qw), 0.5 * NEG, F32), jnp.zeros((1, qw), F32))
    _, _, _, tau, n_gt = lax.while_loop(lambda c: jnp.min(c[2]) < 0.5, snap, carry0)
    need = kf - n_gt

    r_i = lax.broadcasted_iota(jnp.int32, (kb_rows, kb_rows), 0)
    c_i = lax.broadcasted_iota(jnp.int32, (kb_rows, kb_rows), 1)
    tri = jnp.where(c_i <= r_i, 1.0, 0.0).astype(BF16)

    def select_body(kbs, off):
        for kb in kbs:
            s = s_ref[kb]
            eq = s == tau
            rank = _dot(tri, jnp.where(eq, 1.0, 0.0).astype(BF16))
            y = jnp.where(eq, rank + off, jnp.where(s > tau, 0.0, BIG))
            s_ref[kb] = jnp.where(y <= need, 0.0, NEG)
            off = off + rank[kb_rows - 1:kb_rows, :]
        return off

    over_blocks(select_body, jnp.zeros((1, qw), F32))

    pair_w = 2 * HEAD_DIM
    pr = lax.broadcasted_iota(jnp.int32, (pair_w, qw), 0)
    q_pairs = []
    for pair in range(N_HEADS // 2):
        qp = q_ref[0, pair * pair_w:(pair + 1) * pair_w, :]
        zero = jnp.zeros_like(qp)
        q_pairs.append(jnp.concatenate([jnp.where(pr < HEAD_DIM, qp, zero),
                                        jnp.where(pr >= HEAD_DIM, qp, zero)], axis=1))

    ot_ref[...] = jnp.zeros(ot_ref.shape, F32)
    head_rows = [slice(hd * HEAD_DIM, (hd + 1) * HEAD_DIM) for hd in range(N_HEADS)]

    def logit_body(kbs, m8s):
        m8s = list(m8s)
        for kb in kbs:
            bias = s_ref[kb]
            for pair in range(N_HEADS // 2):
                lg2 = _dot(k_blk(kb, slice(pair * pair_w, (pair + 1) * pair_w)), q_pairs[pair])
                for half in range(2):
                    hd = 2 * pair + half
                    lg = lg2[:, half * qw:(half + 1) * qw] + bias
                    lg_ref[hd, kb] = lg
                    m8s[hd] = jnp.maximum(m8s[hd], _slab_reduce(lg, jnp.max))
        return tuple(m8s)

    m8s = over_blocks(logit_body, tuple(jnp.full((SUBLANES, qw), -BIG, F32) for _ in range(N_HEADS)))
    ms = [jnp.max(m8, axis=0, keepdims=True) for m8 in m8s]

    def pv_body(kbs, l8s):
        l8s = list(l8s)
        for hd in range(N_HEADS):
            rows = head_rows[hd]
            acc = ot_ref[rows, :]
            for kb in kbs:
                e = jnp.exp2(lg_ref[hd, kb] - ms[hd])
                acc = acc + _dot(v_blk(kb, rows), e.astype(BF16))
                l8s[hd] = l8s[hd] + _slab_reduce(e, jnp.sum)
            ot_ref[rows, :] = acc
        return tuple(l8s)

    l8s = over_blocks(pv_body, tuple(jnp.zeros((SUBLANES, qw), F32) for _ in range(N_HEADS)))
    for hd in range(N_HEADS):
        rows = head_rows[hd]
        ot_ref[rows, :] = ot_ref[rows, :] / jnp.sum(l8s[hd], axis=0, keepdims=True)

    o_ref[0] = ot_ref[...].T.astype(BF16)


def _attend_call(qi_t, wi_t, q_t, keys, *, layer=None, qw, kb_rows, n_kb_max, n_kb_static,
                 q_pos0, n_valid, top_k):
    b, _, t_q = q_t.shape
    n_q = t_q // qw
    assemble = layer is not None
    qspec = lambda rows: pl.BlockSpec((1, rows, qw), lambda i, j: (i, 0, j))
    scratch = [pltpu.VMEM((n_kb_max, kb_rows, qw), F32), pltpu.VMEM((N_HEADS, n_kb_max, kb_rows, qw), F32),
               pltpu.VMEM((ATT_W, qw), F32)]
    if assemble:
        kic, kc, vtc, kin, kn, vtn = keys
        cache = lambda a: pl.BlockSpec((None, None) + a.shape[2:], lambda i, j: (layer, i, 0, 0))
        fresh = lambda a: pl.BlockSpec((None,) + a.shape[1:], lambda i, j: (i, 0, 0))
        args = [qi_t, wi_t, kic, kin, kc, kn, q_t, vtc, vtn]
        in_specs = [qspec(IDX_W), qspec(SUBLANES), cache(kic), fresh(kin), cache(kc), fresh(kn),
                    qspec(ATT_W), cache(vtc), fresh(vtn)]
        scratch += [pltpu.VMEM((1, kb_rows, IDX_DIM), BF16), pltpu.VMEM((1, kb_rows, ATT_W), BF16),
                    pltpu.VMEM((1, ATT_W, kb_rows), BF16)]
    else:
        ki_b, k_b, v_tb = keys
        batch_blk = lambda a: pl.BlockSpec((1,) + a.shape[1:], lambda i, j: (i, 0, 0, 0))
        args = [qi_t, wi_t, ki_b, k_b, q_t, v_tb]
        in_specs = [qspec(IDX_W), qspec(SUBLANES), batch_blk(ki_b), batch_blk(k_b), qspec(ATT_W),
                    batch_blk(v_tb)]
    kern = functools.partial(_attend_kernel, kb_rows=kb_rows, qw=qw, n_kb_static=n_kb_static,
                             q_pos0=q_pos0, n_valid=n_valid, top_k=top_k, assemble=assemble)
    return pl.pallas_call(
        kern,
        out_shape=jax.ShapeDtypeStruct((b, t_q, ATT_W), BF16),
        grid=(b, n_q),
        in_specs=in_specs,
        out_specs=pl.BlockSpec((1, qw, ATT_W), lambda i, j: (i, j, 0)),
        scratch_shapes=scratch,
        compiler_params=pltpu.CompilerParams(dimension_semantics=("arbitrary", "arbitrary"),
                                             vmem_limit_bytes=VMEM_LIMIT),
        name="attend_cached" if assemble else "attend",
    )(*args)


def _rope_tables(pos):
    half = HEAD_DIM // 2
    freqs = ROPE_THETA ** (-jnp.arange(half, dtype=F32) / half)
    ang = pos.astype(F32)[:, None] * freqs[None, :]
    cos, sin = jnp.cos(ang), jnp.sin(ang)
    cos_h = jnp.concatenate([cos, cos], axis=-1)
    sin_h = jnp.concatenate([-sin, sin], axis=-1)
    n = pos.shape[0]
    pad_c = jnp.ones((n, LANES - IDX_DIM), F32)
    pad_s = jnp.zeros((n, LANES - IDX_DIM), F32)
    return (jnp.tile(cos_h, (1, N_HEADS)), jnp.tile(sin_h, (1, N_HEADS)),
            jnp.concatenate([cos_h, pad_c], axis=-1), jnp.concatenate([sin_h, pad_s], axis=-1))


def _prep_weights(ffn1_norm, ffn1_up, ffn1_down, mix_norm, w_in, conv_w, w_branch, w_out,
                  ffn2_norm, ffn2_up, ffn2_down):
    d = w_in.shape[1]
    c = lambda a: a.astype(BF16)
    gain = lambda a: a[:, None, :]
    sizes = (3 * ATT_W, IDX_W, IDX_DIM + IDX_HEADS, 3 * CONV_W, d, d)
    offs = [0]
    for s in sizes:
        offs.append(offs[-1] + s)
    wqkv, wqi, wkw, wconv, wga, wgb = (w_in[:, :, offs[n]:offs[n + 1]] for n in range(len(sizes)))
    wkw = jnp.pad(wkw, ((0, 0), (0, 0), (0, LANES - IDX_DIM - IDX_HEADS)))
    return dict(
        ffn1_g=gain(ffn1_norm), ffn1_up=c(ffn1_up), ffn1_dn=c(ffn1_down),
        mix_g=gain(mix_norm), wqkv=c(wqkv), wqi=c(wqi), wkw=c(wkw), wconv=c(wconv),
        wga=c(wga), wgb=c(wgb), conv_w=conv_w,
        wb1=c(w_branch[:, :ATT_W]), wb2=c(w_branch[:, ATT_W:]), wout=c(w_out),
        ffn2_g=gain(ffn2_norm), ffn2_up=c(ffn2_up), ffn2_dn=c(ffn2_down),
    )


def _pick_tile(n, pref):
    t = min(pref, n)
    while n % t:
        t //= 2
    return t


def kernel(x_prompt, x_sample, cache_k, cache_v, cache_kidx, state_conv, ffn1_norm, ffn1_up, ffn1_down, mix_norm, w_in, conv_w, w_branch, w_out, ffn2_norm, ffn2_up, ffn2_down, final_norm):
    depth = w_in.shape[0]
    bp, tp, d = x_prompt.shape
    bs, ts, _ = x_sample.shape
    past = cache_k.shape[2]
    assert past % LANES == 0 and ts % (2 * SUBLANES) == 0
    qw_p = 2 * LANES
    qw_s = LANES
    l_s = past + ts
    kb_s = -(-l_s // LANES) * LANES
    topk_p = min(TOPK_MAX, tp // 4)
    topk_s = min(TOPK_MAX, l_s // 4)

    w = _prep_weights(ffn1_norm, ffn1_up, ffn1_down, mix_norm, w_in, conv_w, w_branch, w_out,
                      ffn2_norm, ffn2_up, ffn2_down)
    tabs_p = _rope_tables(jnp.arange(tp))
    tabs_s = _rope_tables(past + jnp.arange(ts))
    fin = final_norm[None]
    zero_conv = jnp.zeros((bp, CONV_K - 1, CONV_W), F32)
    tm_p = _pick_tile(bp * tp, 512)
    tm_s = _pick_tile(bs * ts, 512)
    tm_proj = 2 * qw_p if tp % (2 * qw_p) == 0 else qw_p

    kc_all = jnp.swapaxes(cache_k.reshape(depth, bs, past, ATT_W), 2, 3)
    vtc_all = jnp.swapaxes(cache_v.reshape(depth, bs, past, ATT_W), 2, 3)
    kic_all = cache_kidx.astype(BF16)

    xp = x_prompt.reshape(bp * tp, d)
    xs = x_sample.reshape(bs * ts, d)
    outs = [[] for _ in range(8)]
    for i in range(depth):
        last = i == depth - 1

        xp = _ffn_call(xp, w, i, "ffn1", tm=tm_p)
        (kf, vf, kif, kb, kib, qt, vtb, qit, wit, gbyb, nconv) = _proj_call(
            xp.reshape(bp, tp, d), w, i, zero_conv, tabs_p, tm=tm_proj, kb_rows=qw_p, transposed=True)
        oatt = _attend_call(qit, wit, qt, (kib, kb, vtb), qw=qw_p, kb_rows=qw_p, n_kb_max=tp // qw_p,
                            n_kb_static=None, q_pos0=0, n_valid=tp, top_k=topk_p)
        xp = _ffn_call(xp, w, i, "ffn2", mix=(oatt.reshape(bp * tp, ATT_W), gbyb.reshape(bp * tp, d)),
                       final_g=fin if last else None, tm=tm_p)
        for lst, a in zip(outs[:4], (kf, vf, kif, nconv)):
            lst.append(a)

        xs = _ffn_call(xs, w, i, "ffn1", tm=tm_s)
        (kf, vf, kif, kb, kib, qn, vn, qin, win, gbyb, nconv) = _proj_call(
            xs.reshape(bs, ts, d), w, i, state_conv[i], tabs_s, tm=ts, kb_rows=ts, transposed=False)
        lane_pad = lambda a, width: jnp.pad(jnp.swapaxes(a, 1, 2), ((0, 0), (0, 0), (0, width - ts)))
        oatt = _attend_call(lane_pad(qin, qw_s), lane_pad(win[:, :, IDX_DIM:IDX_DIM + SUBLANES], qw_s),
                            lane_pad(qn, qw_s),
                            (kic_all, kc_all, vtc_all, kib[:, 0], kb[:, 0], lane_pad(vn, kb_s - past)),
                            layer=i, qw=qw_s, kb_rows=kb_s, n_kb_max=1, n_kb_static=1,
                            q_pos0=past, n_valid=l_s, top_k=topk_s)[:, :ts]
        xs = _ffn_call(xs, w, i, "ffn2", mix=(oatt.reshape(bs * ts, ATT_W), gbyb.reshape(bs * ts, d)),
                       final_g=fin if last else None, tm=tm_s)
        for lst, a in zip(outs[4:], (kf, vf, kif, nconv)):
            lst.append(a)

    heads = lambda a: a.reshape(a.shape[:-1] + (N_HEADS, HEAD_DIM))
    st = [jnp.stack(l) for l in outs]
    return (xp.reshape(bp, tp, d), xs.reshape(bs, ts, d),
            heads(st[0]), heads(st[1]), st[2], st[3],
            heads(st[4]), heads(st[5]), st[6], st[7])
```

```python
import functools
import math

import jax
import jax.numpy as jnp
from jax import lax
from jax.experimental import pallas as pl
from jax.experimental.pallas import tpu as pltpu

CHUNK = 64
N_HEADS = 8
HEAD_DIM = 64
ATT_W = N_HEADS * HEAD_DIM
IDX_HEADS = 4
IDX_DIM = 64
IDX_W = IDX_HEADS * IDX_DIM
TOPK_MAX = 256
CONV_W = 512
CONV_K = 3
ROPE_THETA = 10000.0
EPS = 1e-6
NEG = -1e30
BIG = 3e38
IDX_SCALE = (IDX_DIM ** -0.5) * (IDX_HEADS ** -0.5)
Q_SCALE = (HEAD_DIM ** -0.5) * math.log2(math.e)

LANES = 128
SUBLANES = 8
MXU_W = 256
VMEM_LIMIT = 56 * 1024 * 1024
N_BISECT = 20

BF16 = jnp.bfloat16
F32 = jnp.float32


def _dot(a, b):
    return jnp.dot(a, b, preferred_element_type=F32)


def _rms(x, g):
    return x * lax.rsqrt(jnp.mean(x * x, axis=-1, keepdims=True) + EPS) * g


def _const_spec(shape):
    nd = len(shape)
    return pl.BlockSpec(shape, lambda *_: (0,) * nd, pipeline_mode=pl.Buffered(1))


def _layer_spec(a, layer):
    nd = a.ndim
    return pl.BlockSpec((None,) + a.shape[1:], lambda *_: (layer,) + (0,) * (nd - 1),
                        pipeline_mode=pl.Buffered(1))


def _ffn_kernel(*refs, d_ff, chunks, has_mix, has_final):
    it = iter(refs)
    x_ref = next(it)
    if has_mix:
        oatt_ref, gbyb_ref, mixg_ref, wga_ref, wb1_ref, wout_ref = (next(it) for _ in range(6))
    g_ref, wup_ref, wdn_ref = next(it), next(it), next(it)
    fin_ref = next(it) if has_final else None
    o_ref = next(it)

    x = x_ref[...]
    if has_mix:
        hm = _rms(x, mixg_ref[...]).astype(BF16)
        g_a = jax.nn.sigmoid(_dot(hm, wga_ref[...]))
        y_a = _dot(oatt_ref[...], wb1_ref[...])
        mixed = g_a * y_a + gbyb_ref[...].astype(F32)
        x = x + _dot(mixed.astype(BF16), wout_ref[...])
    h = _rms(x, g_ref[...]).astype(BF16)
    acc = jnp.zeros(x.shape, F32)
    for c0, c1 in chunks:
        a = _dot(h, wup_ref[:, c0:c1])
        b = _dot(h, wup_ref[:, d_ff + c0:d_ff + c1])
        act = (a * jax.nn.sigmoid(a) * b).astype(BF16)
        acc = acc + _dot(act, wdn_ref[c0:c1, :])
    y = x + 0.5 * acc
    if has_final:
        y = _rms(y, fin_ref[...])
    o_ref[...] = y


def _ffn_call(x, w, layer, which, *, mix=None, final_g=None, tm):
    n, d = x.shape
    g, w_up, w_dn = w[which + "_g"], w[which + "_up"], w[which + "_dn"]
    d_ff = w_dn.shape[1]
    tiles = d_ff // MXU_W
    split = (tiles + 1) // 2 * MXU_W
    chunks = ((0, split), (split, d_ff)) if d_ff % MXU_W == 0 and tiles >= 2 else ((0, d_ff),)
    row = lambda width: pl.BlockSpec((tm, width), lambda i: (i, 0))
    args, specs = [x], [row(d)]
    if mix is not None:
        oatt, gbyb = mix
        consts = [w["mix_g"], w["wga"], w["wb1"], w["wout"]]
        args += [oatt, gbyb] + consts
        specs += [row(oatt.shape[1]), row(d)] + [_layer_spec(c, layer) for c in consts]
    args += [g, w_up, w_dn]
    specs += [_layer_spec(c, layer) for c in (g, w_up, w_dn)]
    if final_g is not None:
        args.append(final_g)
        specs.append(_const_spec(final_g.shape))
    kern = functools.partial(_ffn_kernel, d_ff=d_ff, chunks=chunks,
                             has_mix=mix is not None, has_final=final_g is not None)
    return pl.pallas_call(
        kern,
        out_shape=jax.ShapeDtypeStruct((n, d), F32),
        grid=(n // tm,),
        in_specs=specs,
        out_specs=row(d),
        compiler_params=pltpu.CompilerParams(dimension_semantics=("arbitrary",),
                                             vmem_limit_bytes=VMEM_LIMIT),
        name="ffn_mix" if mix is not None else "ffn",
    )(*args)


def _rope(y, c, s):
    n = y.shape[-1]
    lane = lax.broadcasted_iota(jnp.int32, y.shape, 1)
    first = (lane & (HEAD_DIM - 1)) < HEAD_DIM // 2
    rot = jnp.where(first, pltpu.roll(y, n - HEAD_DIM // 2, 1), pltpu.roll(y, HEAD_DIM // 2, 1))
    return y * c + rot * s


def _proj_kernel(x_ref, g_ref, wqkv_ref, wqi_ref, wkw_ref, wconv_ref, wgb_ref, wb2_ref,
                 convw_ref, cprev_ref, cos_ref, sin_ref, coskw_ref, sinkw_ref,
                 kf_ref, vf_ref, kif_ref, kb_ref, kib_ref, q_ref, v_ref, qi_ref, wi_ref,
                 gbyb_ref, nconv_ref, ubuf, *, tm, kb_rows):
    t = pl.program_id(1)
    h = _rms(x_ref[0], g_ref[...]).astype(BF16)
    cosf, sinf = cos_ref[...], sin_ref[...]

    q = _rope(_dot(h, wqkv_ref[:, 0:ATT_W]), cosf, sinf) * Q_SCALE
    k = _rope(_dot(h, wqkv_ref[:, ATT_W:2 * ATT_W]), cosf, sinf)
    v = _dot(h, wqkv_ref[:, 2 * ATT_W:3 * ATT_W])
    qi = _rope(_dot(h, wqi_ref[...]), cosf[:, :IDX_W], sinf[:, :IDX_W])
    kw = _rope(_dot(h, wkw_ref[...]), coskw_ref[...], sinkw_ref[...])

    kf_ref[0] = k
    vf_ref[0] = v
    kif_ref[0] = kw[:, :IDX_DIM]
    n_blk = tm // kb_rows
    kb_ref[0] = k.astype(BF16).reshape(n_blk, kb_rows, ATT_W)
    kib_ref[0] = kw[:, :IDX_DIM].astype(BF16).reshape(n_blk, kb_rows, IDX_DIM)

    def tokens_to_lanes(y):
        if tm < LANES:
            y = jnp.concatenate([y, jnp.zeros((LANES - tm, y.shape[1]), y.dtype)], axis=0)
        return y.T

    q_ref[0] = tokens_to_lanes(q).astype(BF16)
    vt = tokens_to_lanes(v).astype(BF16)
    if tm < LANES:
        v_ref[0, 0] = vt
    else:
        for n in range(n_blk):
            v_ref[0, n] = vt[:, n * kb_rows:(n + 1) * kb_rows]
    qi_ref[0] = tokens_to_lanes(qi).astype(BF16)
    wi_ref[0] = tokens_to_lanes(kw)[IDX_DIM:IDX_DIM + SUBLANES, :]

    cb = _dot(h, wconv_ref[:, 0:CONV_W])
    cc = _dot(h, wconv_ref[:, CONV_W:2 * CONV_W])
    cx = _dot(h, wconv_ref[:, 2 * CONV_W:3 * CONV_W])
    u = cc * cx

    @pl.when(t == 0)
    def _():
        ubuf[SUBLANES - 2:SUBLANES, :] = cprev_ref[0]

    ubuf[SUBLANES:SUBLANES + tm, :] = u
    um1 = ubuf[SUBLANES - 1:SUBLANES - 1 + tm, :]
    um2 = ubuf[SUBLANES - 2:SUBLANES - 2 + tm, :]
    cw = convw_ref[...]
    y_conv = cw[0:1, :] * um2 + cw[1:2, :] * um1 + cw[2:3, :] * u
    tail = u[tm - 2:tm, :]
    ubuf[SUBLANES - 2:SUBLANES, :] = tail
    nconv_ref[0] = tail

    y_b = _dot((cb * y_conv).astype(BF16), wb2_ref[...])
    g_b = jax.nn.sigmoid(_dot(h, wgb_ref[...]))
    gbyb_ref[0] = (g_b * y_b).astype(BF16)


def _proj_call(x, w, layer, conv_prev, tabs, *, tm, kb_rows):
    b, t, d = x.shape
    nt = t // tm
    cos, sin, coskw, sinkw = tabs
    consts = [w["mix_g"], w["wqkv"], w["wqi"], w["wkw"], w["wconv"], w["wgb"], w["wb2"], w["conv_w"]]
    tile3 = lambda width: pl.BlockSpec((1, tm, width), lambda i, j: (i, j, 0))
    n_blk = tm // kb_rows
    tile4 = lambda width: pl.BlockSpec((1, n_blk, kb_rows, width), lambda i, j: (i, j, 0, 0))
    tab = lambda a: pl.BlockSpec((tm, a.shape[1]), lambda i, j: (j, 0))
    in_specs = ([tile3(d)] + [_layer_spec(c, layer) for c in consts]
                + [pl.BlockSpec((1, CONV_K - 1, CONV_W), lambda i, j: (i, 0, 0))]
                + [tab(cos), tab(sin), tab(coskw), tab(sinkw)])
    out_shape = [
        jax.ShapeDtypeStruct((b, t, ATT_W), F32),
        jax.ShapeDtypeStruct((b, t, ATT_W), F32),
        jax.ShapeDtypeStruct((b, t, IDX_DIM), F32),
        jax.ShapeDtypeStruct((b, t // kb_rows, kb_rows, ATT_W), BF16),
        jax.ShapeDtypeStruct((b, t // kb_rows, kb_rows, IDX_DIM), BF16),
    ]
    out_specs = [tile3(ATT_W), tile3(ATT_W), tile3(IDX_DIM), tile4(ATT_W), tile4(IDX_DIM)]
    tl = max(tm, LANES)
    vt_blocks, vt_cols = ((nt, tl), (1, tl)) if tm < LANES else ((t // kb_rows, kb_rows), (n_blk, kb_rows))
    lanes3 = lambda rows: pl.BlockSpec((1, rows, tl), lambda i, j: (i, 0, j))
    out_shape += [
        jax.ShapeDtypeStruct((b, ATT_W, nt * tl), BF16),
        jax.ShapeDtypeStruct((b, vt_blocks[0], ATT_W, vt_blocks[1]), BF16),
        jax.ShapeDtypeStruct((b, IDX_W, nt * tl), BF16),
        jax.ShapeDtypeStruct((b, SUBLANES, nt * tl), F32),
    ]
    out_specs += [lanes3(ATT_W),
                  pl.BlockSpec((1, vt_cols[0], ATT_W, vt_cols[1]), lambda i, j: (i, j, 0, 0)),
                  lanes3(IDX_W), lanes3(SUBLANES)]
    out_shape += [
        jax.ShapeDtypeStruct((b, t, d), BF16),
        jax.ShapeDtypeStruct((b, CONV_K - 1, CONV_W), F32),
    ]
    out_specs += [tile3(d), pl.BlockSpec((1, CONV_K - 1, CONV_W), lambda i, j: (i, 0, 0))]
    kern = functools.partial(_proj_kernel, tm=tm, kb_rows=kb_rows)
    return pl.pallas_call(
        kern,
        out_shape=out_shape,
        grid=(b, nt),
        in_specs=in_specs,
        out_specs=out_specs,
        scratch_shapes=[pltpu.VMEM((tm + SUBLANES, CONV_W), F32)],
        compiler_params=pltpu.CompilerParams(dimension_semantics=("arbitrary", "arbitrary"),
                                             vmem_limit_bytes=VMEM_LIMIT),
        name="proj",
    )(x, *consts, conv_prev, cos, sin, coskw, sinkw)


def _slab_reduce(x, op):
    kb, qw = x.shape
    return op(x.reshape(kb // SUBLANES, SUBLANES, qw), axis=0)


def _attend_kernel(*refs, kb_rows, qw, n_kb_static, q_pos0, n_valid, top_k, assemble):
    j = pl.program_id(1)
    n_kb = (j + 1) if n_kb_static is None else n_kb_static
    kf = float(top_k)

    if assemble:
        (qi_ref, wi_ref, kic_ref, kin_ref, kc_ref, kn_ref, q_ref, vtc_ref, vtn_ref,
         o_ref, s_ref, lg_ref, ot_ref, ki_s, k_s, v_s) = refs
        past, new = kc_ref.shape[1], kn_ref.shape[0]
        for dst, cached, fresh in ((ki_s, kic_ref[...], kin_ref), (k_s, kc_ref[...].T.astype(BF16), kn_ref)):
            dst[0, 0:past, :] = cached
            dst[0, past:past + new, :] = fresh[...]
            dst[0, past + new:, :] = jnp.zeros((kb_rows - past - new, dst.shape[2]), BF16)
        v_s[0, :, 0:past] = vtc_ref[...].astype(BF16)
        v_s[0, :, past:] = vtn_ref[...]
        ki_blk = lambda kb: ki_s[kb]
        k_blk = lambda kb, lanes: k_s[kb, :, lanes]
        v_blk = lambda kb, rows: v_s[kb, rows, :]
    else:
        qi_ref, wi_ref, ki_ref, k_ref, q_ref, v_ref, o_ref, s_ref, lg_ref, ot_ref = refs
        ki_blk = lambda kb: ki_ref[0, kb]
        k_blk = lambda kb, lanes: k_ref[0, kb, :, lanes]
        v_blk = lambda kb, rows: v_ref[0, kb, rows, :]

    def over_blocks(body, init, n=None, scores_only=False):
        n = n_kb if n is None else n
        pair = lambda i, c: body([2 * i, 2 * i + 1], c)
        if isinstance(n, int):
            pairs = lax.fori_loop(0, n // 2, pair, init)
            return body([n - 1], pairs) if n % 2 else pairs
        if scores_only:
            return lax.fori_loop(0, (n + 1) // 2, pair, init)
        pairs = lax.fori_loop(0, n // 2, pair, init)
        return lax.cond(n % 2 == 1, lambda c: body([n - 1], c), lambda c: c, pairs)

    last = n_kb - 1
    row = lax.broadcasted_iota(jnp.int32, (kb_rows, qw), 0) + last * kb_rows
    qpos = lax.broadcasted_iota(jnp.int32, (kb_rows, qw), 1) + (q_pos0 + j * qw)
    chunk_shift = CHUNK.bit_length() - 1
    adm_last = ((row >> chunk_shift) <= (qpos >> chunk_shift)) & (row < n_valid)

    def score_block(kb):
        ki = ki_blk(kb)
        s = jnp.zeros((kb_rows, qw), F32)
        for hh in range(IDX_HEADS):
            d = _dot(ki, qi_ref[0, hh * IDX_DIM:(hh + 1) * IDX_DIM, :])
            s = s + jnp.maximum(d, 0.0) * wi_ref[0, hh:hh + 1, :]
        return s * IDX_SCALE

    def score_body(kbs, carry):
        mx, mn = carry
        for kb in kbs:
            s = score_block(kb)
            s_ref[kb] = s
            mx = jnp.maximum(mx, _slab_reduce(s, jnp.max))
            mn = jnp.minimum(mn, _slab_reduce(s, jnp.min))
        return mx, mn

    mx8 = jnp.full((SUBLANES, qw), -BIG, F32)
    mn8 = jnp.full((SUBLANES, qw), BIG, F32)
    mx8, mn8 = over_blocks(score_body, (mx8, mn8), n=last)
    s_last = score_block(last)
    s_ref[last] = jnp.where(adm_last, s_last, NEG)
    if n_kb_static is None:
        @pl.when(n_kb % 2 == 1)
        def _():
            s_ref[n_kb] = jnp.full((kb_rows, qw), NEG, F32)
    mx8 = jnp.maximum(mx8, _slab_reduce(jnp.where(adm_last, s_last, -BIG), jnp.max))
    mn8 = jnp.minimum(mn8, _slab_reduce(jnp.where(adm_last, s_last, BIG), jnp.min))
    smax = jnp.max(mx8, axis=0, keepdims=True)
    smin = jnp.min(mn8, axis=0, keepdims=True)
    n_adm = (jnp.sum(_slab_reduce(jnp.where(adm_last, 1.0, 0.0), jnp.sum), axis=0, keepdims=True)
             + jnp.float32(1.0) * (last * kb_rows))

    def count(pred_fn):
        def body(kbs, acc):
            for kb in kbs:
                acc = acc + _slab_reduce(jnp.where(pred_fn(s_ref[kb]), 1.0, 0.0), jnp.sum)
            return acc
        acc8 = over_blocks(body, jnp.zeros((SUBLANES, qw), F32), scores_only=True)
        return jnp.sum(acc8, axis=0, keepdims=True)

    def bisect(_, carry):
        lo, hi, c_hi = carry
        mid = 0.5 * lo + 0.5 * hi
        cnt = count(lambda s: s >= mid)
        ge = cnt >= kf
        return jnp.where(ge, mid, lo), jnp.where(ge, hi, mid), jnp.where(ge, c_hi, cnt)

    hi0 = smax + (jnp.abs(smax) + 1.0) * (2.0 ** -10)
    _, hi, c_hi = lax.fori_loop(0, N_BISECT, bisect, (smin, hi0, jnp.zeros((1, qw), F32)))

    def snap(carry):
        hi, c_hi, done, tau, n_gt = carry

        def vbody(kbs, acc):
            for kb in kbs:
                s = s_ref[kb]
                acc = jnp.maximum(acc, _slab_reduce(jnp.where(s < hi, s, -BIG), jnp.max))
            return acc
        cand8 = over_blocks(vbody, jnp.full((SUBLANES, qw), -BIG, F32), scores_only=True)
        cand = jnp.max(cand8, axis=0, keepdims=True)
        c_ge = count(lambda s: s >= cand)
        ok = c_ge >= kf
        fresh = ok & (done < 0.5)
        return (jnp.where(ok, hi, cand), jnp.where(ok, c_hi, c_ge), jnp.where(ok, 1.0, done),
                jnp.where(fresh, cand, tau), jnp.where(fresh, c_hi, n_gt))

    few = n_adm <= kf
    carry0 = (hi, c_hi, jnp.where(few, 1.0, 0.0), jnp.full((1, qw), 0.5 * NEG, F32), jnp.zeros((1, qw), F32))
    _, _, _, tau, n_gt = lax.while_loop(lambda c: jnp.min(c[2]) < 0.5, snap, carry0)
    need = kf - n_gt

    r_i = lax.broadcasted_iota(jnp.int32, (LANES, LANES), 0)
    c_i = lax.broadcasted_iota(jnp.int32, (LANES, LANES), 1)
    tri = jnp.where(c_i <= r_i, 1.0, 0.0).astype(BF16)

    def select_body(kbs, off):
        for kb in kbs:
            for r0 in range(0, kb_rows, LANES):
                s = s_ref[kb, r0:r0 + LANES, :]
                eq = s == tau
                rank = _dot(tri, jnp.where(eq, 1.0, 0.0).astype(BF16))
                y = jnp.where(eq, rank + off, jnp.where(s > tau, 0.0, BIG))
                s_ref[kb, r0:r0 + LANES, :] = jnp.where(y <= need, 0.0, NEG)
                off = off + rank[LANES - 1:LANES, :]
        return off

    over_blocks(select_body, jnp.zeros((1, qw), F32))

    pair_w = 2 * HEAD_DIM
    pr = lax.broadcasted_iota(jnp.int32, (pair_w, qw), 0)
    q_pairs = []
    for pair in range(N_HEADS // 2):
        qp = q_ref[0, pair * pair_w:(pair + 1) * pair_w, :]
        zero = jnp.zeros_like(qp)
        q_pairs.append(jnp.concatenate([jnp.where(pr < HEAD_DIM, qp, zero),
                                        jnp.where(pr >= HEAD_DIM, qp, zero)], axis=1))

    ot_ref[...] = jnp.zeros(ot_ref.shape, F32)
    head_rows = [slice(hd * HEAD_DIM, (hd + 1) * HEAD_DIM) for hd in range(N_HEADS)]

    def logit_body(kbs, m8s):
        m8s = list(m8s)
        for kb in kbs:
            bias = s_ref[kb]
            for pair in range(N_HEADS // 2):
                lg2 = _dot(k_blk(kb, slice(pair * pair_w, (pair + 1) * pair_w)), q_pairs[pair])
                for half in range(2):
                    hd = 2 * pair + half
                    lg = lg2[:, half * qw:(half + 1) * qw] + bias
                    lg_ref[hd, kb] = lg
                    m8s[hd] = jnp.maximum(m8s[hd], _slab_reduce(lg, jnp.max))
        return tuple(m8s)

    m8s = over_blocks(logit_body, tuple(jnp.full((SUBLANES, qw), -BIG, F32) for _ in range(N_HEADS)))
    ms = [jnp.max(m8, axis=0, keepdims=True) for m8 in m8s]

    def pv_body(kbs, l8s):
        l8s = list(l8s)
        for hd in range(N_HEADS):
            rows = head_rows[hd]
            acc = ot_ref[rows, :]
            for kb in kbs:
                e = jnp.exp2(lg_ref[hd, kb] - ms[hd])
                acc = acc + _dot(v_blk(kb, rows), e.astype(BF16))
                l8s[hd] = l8s[hd] + _slab_reduce(e, jnp.sum)
            ot_ref[rows, :] = acc
        return tuple(l8s)

    l8s = over_blocks(pv_body, tuple(jnp.zeros((SUBLANES, qw), F32) for _ in range(N_HEADS)))
    for hd in range(N_HEADS):
        rows = head_rows[hd]
        ot_ref[rows, :] = ot_ref[rows, :] / jnp.sum(l8s[hd], axis=0, keepdims=True)

    o_ref[0] = ot_ref[...].T.astype(BF16)


def _attend_call(qi_t, wi_t, q_t, keys, *, layer=None, qw, kb_rows, n_kb_max, n_kb_static,
                 q_pos0, n_valid, top_k):
    b, _, t_q = q_t.shape
    n_q = t_q // qw
    assemble = layer is not None
    qspec = lambda rows: pl.BlockSpec((1, rows, qw), lambda i, j: (i, 0, j))
    s_blocks = n_kb_max + (n_kb_max % 2 if n_kb_static is None else 0)
    scratch = [pltpu.VMEM((s_blocks, kb_rows, qw), F32), pltpu.VMEM((N_HEADS, n_kb_max, kb_rows, qw), F32),
               pltpu.VMEM((ATT_W, qw), F32)]
    if assemble:
        kic, kc, vtc, kin, kn, vtn = keys
        cache = lambda a: pl.BlockSpec((None, None) + a.shape[2:], lambda i, j: (layer, i, 0, 0))
        fresh = lambda a: pl.BlockSpec((None,) + a.shape[1:], lambda i, j: (i, 0, 0))
        args = [qi_t, wi_t, kic, kin, kc, kn, q_t, vtc, vtn]
        in_specs = [qspec(IDX_W), qspec(SUBLANES), cache(kic), fresh(kin), cache(kc), fresh(kn),
                    qspec(ATT_W), cache(vtc), fresh(vtn)]
        scratch += [pltpu.VMEM((1, kb_rows, IDX_DIM), BF16), pltpu.VMEM((1, kb_rows, ATT_W), BF16),
                    pltpu.VMEM((1, ATT_W, kb_rows), BF16)]
    else:
        ki_b, k_b, v_tb = keys
        batch_blk = lambda a: pl.BlockSpec((1,) + a.shape[1:], lambda i, j: (i, 0, 0, 0))
        args = [qi_t, wi_t, ki_b, k_b, q_t, v_tb]
        in_specs = [qspec(IDX_W), qspec(SUBLANES), batch_blk(ki_b), batch_blk(k_b), qspec(ATT_W),
                    batch_blk(v_tb)]
    kern = functools.partial(_attend_kernel, kb_rows=kb_rows, qw=qw, n_kb_static=n_kb_static,
                             q_pos0=q_pos0, n_valid=n_valid, top_k=top_k, assemble=assemble)
    return pl.pallas_call(
        kern,
        out_shape=jax.ShapeDtypeStruct((b, t_q, ATT_W), BF16),
        grid=(b, n_q),
        in_specs=in_specs,
        out_specs=pl.BlockSpec((1, qw, ATT_W), lambda i, j: (i, j, 0)),
        scratch_shapes=scratch,
        compiler_params=pltpu.CompilerParams(dimension_semantics=("arbitrary", "arbitrary"),
                                             vmem_limit_bytes=VMEM_LIMIT),
        name="attend_cached" if assemble else "attend",
    )(*args)


def _rope_tables(pos):
    half = HEAD_DIM // 2
    freqs = ROPE_THETA ** (-jnp.arange(half, dtype=F32) / half)
    ang = pos.astype(F32)[:, None] * freqs[None, :]
    cos, sin = jnp.cos(ang), jnp.sin(ang)
    cos_h = jnp.concatenate([cos, cos], axis=-1)
    sin_h = jnp.concatenate([-sin, sin], axis=-1)
    n = pos.shape[0]
    pad_c = jnp.ones((n, LANES - IDX_DIM), F32)
    pad_s = jnp.zeros((n, LANES - IDX_DIM), F32)
    return (jnp.tile(cos_h, (1, N_HEADS)), jnp.tile(sin_h, (1, N_HEADS)),
            jnp.concatenate([cos_h, pad_c], axis=-1), jnp.concatenate([sin_h, pad_s], axis=-1))


def _prep_weights(ffn1_norm, ffn1_up, ffn1_down, mix_norm, w_in, conv_w, w_branch, w_out,
                  ffn2_norm, ffn2_up, ffn2_down):
    d = w_in.shape[1]
    c = lambda a: a.astype(BF16)
    gain = lambda a: a[:, None, :]
    sizes = (3 * ATT_W, IDX_W, IDX_DIM + IDX_HEADS, 3 * CONV_W, d, d)
    offs = [0]
    for s in sizes:
        offs.append(offs[-1] + s)
    wqkv, wqi, wkw, wconv, wga, wgb = (w_in[:, :, offs[n]:offs[n + 1]] for n in range(len(sizes)))
    wkw = jnp.pad(wkw, ((0, 0), (0, 0), (0, LANES - IDX_DIM - IDX_HEADS)))
    return dict(
        ffn1_g=gain(ffn1_norm), ffn1_up=c(ffn1_up), ffn1_dn=c(ffn1_down),
        mix_g=gain(mix_norm), wqkv=c(wqkv), wqi=c(wqi), wkw=c(wkw), wconv=c(wconv),
        wga=c(wga), wgb=c(wgb), conv_w=conv_w,
        wb1=c(w_branch[:, :ATT_W]), wb2=c(w_branch[:, ATT_W:]), wout=c(w_out),
        ffn2_g=gain(ffn2_norm), ffn2_up=c(ffn2_up), ffn2_dn=c(ffn2_down),
    )


def _pick_tile(n, pref):
    t = min(pref, n)
    while n % t:
        t //= 2
    return t


def kernel(x_prompt, x_sample, cache_k, cache_v, cache_kidx, state_conv, ffn1_norm, ffn1_up, ffn1_down, mix_norm, w_in, conv_w, w_branch, w_out, ffn2_norm, ffn2_up, ffn2_down, final_norm):
    depth = w_in.shape[0]
    bp, tp, d = x_prompt.shape
    bs, ts, _ = x_sample.shape
    past = cache_k.shape[2]
    assert past % LANES == 0 and ts % (2 * SUBLANES) == 0 and ts <= LANES
    qw_p = 2 * LANES
    qw_s = LANES
    l_s = past + ts
    kb_s = -(-l_s // LANES) * LANES
    topk_p = min(TOPK_MAX, tp // 4)
    topk_s = min(TOPK_MAX, l_s // 4)

    w = _prep_weights(ffn1_norm, ffn1_up, ffn1_down, mix_norm, w_in, conv_w, w_branch, w_out,
                      ffn2_norm, ffn2_up, ffn2_down)
    tabs_p = _rope_tables(jnp.arange(tp))
    tabs_s = _rope_tables(past + jnp.arange(ts))
    fin = final_norm[None]
    zero_conv = jnp.zeros((bp, CONV_K - 1, CONV_W), F32)
    tm_p = _pick_tile(bp * tp, 512)
    tm_s = _pick_tile(bs * ts, 512)
    tm_proj = 2 * qw_p if tp % (2 * qw_p) == 0 else qw_p

    kc_all = jnp.swapaxes(cache_k.reshape(depth, bs, past, ATT_W), 2, 3)
    vtc_all = jnp.swapaxes(cache_v.reshape(depth, bs, past, ATT_W), 2, 3)
    kic_all = cache_kidx.astype(BF16)

    xp = x_prompt.reshape(bp * tp, d)
    xs = x_sample.reshape(bs * ts, d)
    outs = [[] for _ in range(8)]
    for i in range(depth):
        last = i == depth - 1

        xp = _ffn_call(xp, w, i, "ffn1", tm=tm_p)
        (kf, vf, kif, kb, kib, qt, vtb, qit, wit, gbyb, nconv) = _proj_call(
            xp.reshape(bp, tp, d), w, i, zero_conv, tabs_p, tm=tm_proj, kb_rows=qw_p)
        oatt = _attend_call(qit, wit, qt, (kib, kb, vtb), qw=qw_p, kb_rows=qw_p, n_kb_max=tp // qw_p,
                            n_kb_static=None, q_pos0=0, n_valid=tp, top_k=topk_p)
        xp = _ffn_call(xp, w, i, "ffn2", mix=(oatt.reshape(bp * tp, ATT_W), gbyb.reshape(bp * tp, d)),
                       final_g=fin if last else None, tm=tm_p)
        for lst, a in zip(outs[:4], (kf, vf, kif, nconv)):
            lst.append(a)

        xs = _ffn_call(xs, w, i, "ffn1", tm=tm_s)
        (kf, vf, kif, kb, kib, qt, vtb, qit, wit, gbyb, nconv) = _proj_call(
            xs.reshape(bs, ts, d), w, i, state_conv[i], tabs_s, tm=ts, kb_rows=ts)
        oatt = _attend_call(qit, wit, qt, (kic_all, kc_all, vtc_all, kib[:, 0], kb[:, 0], vtb[:, 0]),
                            layer=i, qw=qw_s, kb_rows=kb_s, n_kb_max=1, n_kb_static=1,
                            q_pos0=past, n_valid=l_s, top_k=topk_s)[:, :ts]
        xs = _ffn_call(xs, w, i, "ffn2", mix=(oatt.reshape(bs * ts, ATT_W), gbyb.reshape(bs * ts, d)),
                       final_g=fin if last else None, tm=tm_s)
        for lst, a in zip(outs[4:], (kf, vf, kif, nconv)):
            lst.append(a)

    heads = lambda a: a.reshape(a.shape[:-1] + (N_HEADS, HEAD_DIM))
    st = [jnp.stack(l) for l in outs]
    return (xp.reshape(bp, tp, d), xs.reshape(bs, ts, d),
            heads(st[0]), heads(st[1]), st[2], st[3],
            heads(st[4]), heads(st[5]), st[6], st[7])
```

```python
import functools
import math

import jax
import jax.numpy as jnp
from jax import lax
from jax.experimental import pallas as pl
from jax.experimental.pallas import tpu as pltpu

CHUNK = 64
N_HEADS = 8
HEAD_DIM = 64
ATT_W = N_HEADS * HEAD_DIM
IDX_HEADS = 4
IDX_DIM = 64
IDX_W = IDX_HEADS * IDX_DIM
TOPK_MAX = 256
CONV_W = 512
CONV_K = 3
ROPE_THETA = 10000.0
EPS = 1e-6
NEG = -1e30
BIG = 3e38
IDX_SCALE = (IDX_DIM ** -0.5) * (IDX_HEADS ** -0.5)
Q_SCALE = (HEAD_DIM ** -0.5) * math.log2(math.e)

LANES = 128
SUBLANES = 8
MXU_W = 256
VMEM_LIMIT = 56 * 1024 * 1024
N_BISECT = 18

BF16 = jnp.bfloat16
F32 = jnp.float32


def _dot(a, b):
    return jnp.dot(a, b, preferred_element_type=F32)


def _rms(x, g):
    return x * lax.rsqrt(jnp.mean(x * x, axis=-1, keepdims=True) + EPS) * g


def _const_spec(shape):
    nd = len(shape)
    return pl.BlockSpec(shape, lambda *_: (0,) * nd, pipeline_mode=pl.Buffered(1))


def _layer_spec(a, layer):
    nd = a.ndim
    return pl.BlockSpec((None,) + a.shape[1:], lambda *_: (layer,) + (0,) * (nd - 1),
                        pipeline_mode=pl.Buffered(1))


def _ffn_kernel(*refs, d_ff, chunks, has_mix, has_final):
    it = iter(refs)
    x_ref = next(it)
    if has_mix:
        oatt_ref, gbyb_ref, mixg_ref, wga_ref, wb1_ref, wout_ref = (next(it) for _ in range(6))
    g_ref, wup_ref, wdn_ref = next(it), next(it), next(it)
    fin_ref = next(it) if has_final else None
    o_ref = next(it)

    x = x_ref[...]
    if has_mix:
        hm = _rms(x, mixg_ref[...]).astype(BF16)
        g_a = jax.nn.sigmoid(_dot(hm, wga_ref[...]))
        y_a = _dot(oatt_ref[...], wb1_ref[...])
        mixed = g_a * y_a + gbyb_ref[...].astype(F32)
        x = x + _dot(mixed.astype(BF16), wout_ref[...])
    h = _rms(x, g_ref[...]).astype(BF16)
    acc = jnp.zeros(x.shape, F32)
    for c0, c1 in chunks:
        a = _dot(h, wup_ref[:, c0:c1])
        b = _dot(h, wup_ref[:, d_ff + c0:d_ff + c1])
        act = (a * jax.nn.sigmoid(a) * b).astype(BF16)
        acc = acc + _dot(act, wdn_ref[c0:c1, :])
    y = x + 0.5 * acc
    if has_final:
        y = _rms(y, fin_ref[...])
    o_ref[...] = y


def _ffn_call(x, w, layer, which, *, mix=None, final_g=None, tm):
    n, d = x.shape
    g, w_up, w_dn = w[which + "_g"], w[which + "_up"], w[which + "_dn"]
    d_ff = w_dn.shape[1]
    tiles = d_ff // MXU_W
    split = (tiles + 1) // 2 * MXU_W
    chunks = ((0, split), (split, d_ff)) if d_ff % MXU_W == 0 and tiles >= 2 else ((0, d_ff),)
    row = lambda width: pl.BlockSpec((tm, width), lambda i: (i, 0))
    args, specs = [x], [row(d)]
    if mix is not None:
        oatt, gbyb = mix
        consts = [w["mix_g"], w["wga"], w["wb1"], w["wout"]]
        args += [oatt, gbyb] + consts
        specs += [row(oatt.shape[1]), row(d)] + [_layer_spec(c, layer) for c in consts]
    args += [g, w_up, w_dn]
    specs += [_layer_spec(c, layer) for c in (g, w_up, w_dn)]
    if final_g is not None:
        args.append(final_g)
        specs.append(_const_spec(final_g.shape))
    kern = functools.partial(_ffn_kernel, d_ff=d_ff, chunks=chunks,
                             has_mix=mix is not None, has_final=final_g is not None)
    return pl.pallas_call(
        kern,
        out_shape=jax.ShapeDtypeStruct((n, d), F32),
        grid=(n // tm,),
        in_specs=specs,
        out_specs=row(d),
        compiler_params=pltpu.CompilerParams(dimension_semantics=("arbitrary",),
                                             vmem_limit_bytes=VMEM_LIMIT),
        name="ffn_mix" if mix is not None else "ffn",
    )(*args)


def _rope(y, c, s):
    n = y.shape[-1]
    lane = lax.broadcasted_iota(jnp.int32, y.shape, 1)
    first = (lane & (HEAD_DIM - 1)) < HEAD_DIM // 2
    rot = jnp.where(first, pltpu.roll(y, n - HEAD_DIM // 2, 1), pltpu.roll(y, HEAD_DIM // 2, 1))
    return y * c + rot * s


def _proj_kernel(x_ref, g_ref, wqkv_ref, wqi_ref, wkw_ref, wconv_ref, wgb_ref, wb2_ref,
                 convw_ref, cprev_ref, cos_ref, sin_ref, coskw_ref, sinkw_ref,
                 kf_ref, vf_ref, kif_ref, kb_ref, kib_ref, q_ref, v_ref, qi_ref, wi_ref,
                 gbyb_ref, nconv_ref, ubuf, *, tm, kb_rows):
    t = pl.program_id(1)
    h = _rms(x_ref[0], g_ref[...]).astype(BF16)
    cosf, sinf = cos_ref[...], sin_ref[...]

    q = _rope(_dot(h, wqkv_ref[:, 0:ATT_W]), cosf, sinf) * Q_SCALE
    k = _rope(_dot(h, wqkv_ref[:, ATT_W:2 * ATT_W]), cosf, sinf)
    v = _dot(h, wqkv_ref[:, 2 * ATT_W:3 * ATT_W])
    qi = _rope(_dot(h, wqi_ref[...]), cosf[:, :IDX_W], sinf[:, :IDX_W])
    kw = _rope(_dot(h, wkw_ref[...]), coskw_ref[...], sinkw_ref[...])

    kf_ref[0] = k
    vf_ref[0] = v
    kif_ref[0] = kw[:, :IDX_DIM]
    n_blk = tm // kb_rows
    kb_ref[0] = k.astype(BF16).reshape(n_blk, kb_rows, ATT_W)
    kib_ref[0] = kw[:, :IDX_DIM].astype(BF16).reshape(n_blk, kb_rows, IDX_DIM)

    def tokens_to_lanes(y):
        if tm < LANES:
            y = jnp.concatenate([y, jnp.zeros((LANES - tm, y.shape[1]), y.dtype)], axis=0)
        return y.T

    q_ref[0] = tokens_to_lanes(q).astype(BF16)
    vt = tokens_to_lanes(v).astype(BF16)
    if tm < LANES:
        v_ref[0, 0] = vt
    else:
        for n in range(n_blk):
            v_ref[0, n] = vt[:, n * kb_rows:(n + 1) * kb_rows]
    qi_ref[0] = tokens_to_lanes(qi).astype(BF16)
    wi_ref[0] = tokens_to_lanes(kw)[IDX_DIM:IDX_DIM + SUBLANES, :]

    cb = _dot(h, wconv_ref[:, 0:CONV_W])
    cc = _dot(h, wconv_ref[:, CONV_W:2 * CONV_W])
    cx = _dot(h, wconv_ref[:, 2 * CONV_W:3 * CONV_W])
    u = cc * cx

    @pl.when(t == 0)
    def _():
        ubuf[SUBLANES - 2:SUBLANES, :] = cprev_ref[0]

    ubuf[SUBLANES:SUBLANES + tm, :] = u
    um1 = ubuf[SUBLANES - 1:SUBLANES - 1 + tm, :]
    um2 = ubuf[SUBLANES - 2:SUBLANES - 2 + tm, :]
    cw = convw_ref[...]
    y_conv = cw[0:1, :] * um2 + cw[1:2, :] * um1 + cw[2:3, :] * u
    tail = u[tm - 2:tm, :]
    ubuf[SUBLANES - 2:SUBLANES, :] = tail
    nconv_ref[0] = tail

    y_b = _dot((cb * y_conv).astype(BF16), wb2_ref[...])
    g_b = jax.nn.sigmoid(_dot(h, wgb_ref[...]))
    gbyb_ref[0] = (g_b * y_b).astype(BF16)


def _proj_call(x, w, layer, conv_prev, tabs, *, tm, kb_rows):
    b, t, d = x.shape
    nt = t // tm
    cos, sin, coskw, sinkw = tabs
    consts = [w["mix_g"], w["wqkv"], w["wqi"], w["wkw"], w["wconv"], w["wgb"], w["wb2"], w["conv_w"]]
    tile3 = lambda width: pl.BlockSpec((1, tm, width), lambda i, j: (i, j, 0))
    n_blk = tm // kb_rows
    tile4 = lambda width: pl.BlockSpec((1, n_blk, kb_rows, width), lambda i, j: (i, j, 0, 0))
    tab = lambda a: pl.BlockSpec((tm, a.shape[1]), lambda i, j: (j, 0))
    in_specs = ([tile3(d)] + [_layer_spec(c, layer) for c in consts]
                + [pl.BlockSpec((1, CONV_K - 1, CONV_W), lambda i, j: (i, 0, 0))]
                + [tab(cos), tab(sin), tab(coskw), tab(sinkw)])
    out_shape = [
        jax.ShapeDtypeStruct((b, t, ATT_W), F32),
        jax.ShapeDtypeStruct((b, t, ATT_W), F32),
        jax.ShapeDtypeStruct((b, t, IDX_DIM), F32),
        jax.ShapeDtypeStruct((b, t // kb_rows, kb_rows, ATT_W), BF16),
        jax.ShapeDtypeStruct((b, t // kb_rows, kb_rows, IDX_DIM), BF16),
    ]
    out_specs = [tile3(ATT_W), tile3(ATT_W), tile3(IDX_DIM), tile4(ATT_W), tile4(IDX_DIM)]
    tl = max(tm, LANES)
    vt_blocks, vt_cols = ((nt, tl), (1, tl)) if tm < LANES else ((t // kb_rows, kb_rows), (n_blk, kb_rows))
    lanes3 = lambda rows: pl.BlockSpec((1, rows, tl), lambda i, j: (i, 0, j))
    out_shape += [
        jax.ShapeDtypeStruct((b, ATT_W, nt * tl), BF16),
        jax.ShapeDtypeStruct((b, vt_blocks[0], ATT_W, vt_blocks[1]), BF16),
        jax.ShapeDtypeStruct((b, IDX_W, nt * tl), BF16),
        jax.ShapeDtypeStruct((b, SUBLANES, nt * tl), F32),
    ]
    out_specs += [lanes3(ATT_W),
                  pl.BlockSpec((1, vt_cols[0], ATT_W, vt_cols[1]), lambda i, j: (i, j, 0, 0)),
                  lanes3(IDX_W), lanes3(SUBLANES)]
    out_shape += [
        jax.ShapeDtypeStruct((b, t, d), BF16),
        jax.ShapeDtypeStruct((b, CONV_K - 1, CONV_W), F32),
    ]
    out_specs += [tile3(d), pl.BlockSpec((1, CONV_K - 1, CONV_W), lambda i, j: (i, 0, 0))]
    kern = functools.partial(_proj_kernel, tm=tm, kb_rows=kb_rows)
    return pl.pallas_call(
        kern,
        out_shape=out_shape,
        grid=(b, nt),
        in_specs=in_specs,
        out_specs=out_specs,
        scratch_shapes=[pltpu.VMEM((tm + SUBLANES, CONV_W), F32)],
        compiler_params=pltpu.CompilerParams(dimension_semantics=("arbitrary", "arbitrary"),
                                             vmem_limit_bytes=VMEM_LIMIT),
        name="proj",
    )(x, *consts, conv_prev, cos, sin, coskw, sinkw)


def _slab_reduce(x, op):
    kb, qw = x.shape
    return op(x.reshape(kb // SUBLANES, SUBLANES, qw), axis=0)


def _attend_kernel(*refs, kb_rows, qw, n_kb_static, q_pos0, n_valid, top_k, assemble):
    j = pl.program_id(1)
    n_kb = (j + 1) if n_kb_static is None else n_kb_static
    kf = float(top_k)

    if assemble:
        (qi_ref, wi_ref, kic_ref, kin_ref, kc_ref, kn_ref, q_ref, vtc_ref, vtn_ref,
         o_ref, s_ref, lg_ref, ot_ref, ki_s, k_s, v_s) = refs
        past, new = kc_ref.shape[1], kn_ref.shape[0]
        for dst, cached, fresh in ((ki_s, kic_ref[...], kin_ref), (k_s, kc_ref[...].T.astype(BF16), kn_ref)):
            dst[0, 0:past, :] = cached
            dst[0, past:past + new, :] = fresh[...]
            dst[0, past + new:, :] = jnp.zeros((kb_rows - past - new, dst.shape[2]), BF16)
        v_s[0, :, 0:past] = vtc_ref[...].astype(BF16)
        v_s[0, :, past:] = vtn_ref[...]
        ki_blk = lambda kb: ki_s[kb]
        k_blk = lambda kb, lanes: k_s[kb, :, lanes]
        v_blk = lambda kb, rows: v_s[kb, rows, :]
    else:
        qi_ref, wi_ref, ki_ref, k_ref, q_ref, v_ref, o_ref, s_ref, lg_ref, ot_ref = refs
        ki_blk = lambda kb: ki_ref[0, kb]
        k_blk = lambda kb, lanes: k_ref[0, kb, :, lanes]
        v_blk = lambda kb, rows: v_ref[0, kb, rows, :]

    def over_blocks(body, init, n=None):
        n = n_kb if n is None else n
        pairs = lax.fori_loop(0, n // 2, lambda i, c: body([2 * i, 2 * i + 1], c), init)
        if isinstance(n, int):
            return body([n - 1], pairs) if n % 2 else pairs
        return lax.cond(n % 2 == 1, lambda c: body([n - 1], c), lambda c: c, pairs)

    last = n_kb - 1
    row = lax.broadcasted_iota(jnp.int32, (kb_rows, qw), 0) + last * kb_rows
    qpos = lax.broadcasted_iota(jnp.int32, (kb_rows, qw), 1) + (q_pos0 + j * qw)
    chunk_shift = CHUNK.bit_length() - 1
    adm_last = ((row >> chunk_shift) <= (qpos >> chunk_shift)) & (row < n_valid)

    def score_block(kb):
        ki = ki_blk(kb)
        s = jnp.zeros((kb_rows, qw), F32)
        for hh in range(IDX_HEADS):
            d = _dot(ki, qi_ref[0, hh * IDX_DIM:(hh + 1) * IDX_DIM, :])
            s = s + jnp.maximum(d, 0.0) * wi_ref[0, hh:hh + 1, :]
        return s * IDX_SCALE

    def score_body(kbs, carry):
        mx, mn = carry
        for kb in kbs:
            s = score_block(kb)
            s_ref[kb] = s
            mx = jnp.maximum(mx, _slab_reduce(s, jnp.max))
            mn = jnp.minimum(mn, _slab_reduce(s, jnp.min))
        return mx, mn

    mx8 = jnp.full((SUBLANES, qw), -BIG, F32)
    mn8 = jnp.full((SUBLANES, qw), BIG, F32)
    mx8, mn8 = over_blocks(score_body, (mx8, mn8), n=last)
    s_last = score_block(last)
    s_ref[last] = jnp.where(adm_last, s_last, NEG)
    mx8 = jnp.maximum(mx8, _slab_reduce(jnp.where(adm_last, s_last, -BIG), jnp.max))
    mn8 = jnp.minimum(mn8, _slab_reduce(jnp.where(adm_last, s_last, BIG), jnp.min))
    smax = jnp.max(mx8, axis=0, keepdims=True)
    smin = jnp.min(mn8, axis=0, keepdims=True)
    n_adm = (jnp.sum(_slab_reduce(jnp.where(adm_last, 1.0, 0.0), jnp.sum), axis=0, keepdims=True)
             + jnp.float32(1.0) * (last * kb_rows))

    def count(pred_fn):
        def body(kbs, acc):
            for kb in kbs:
                acc = acc + _slab_reduce(jnp.where(pred_fn(s_ref[kb]), 1.0, 0.0), jnp.sum)
            return acc
        return jnp.sum(over_blocks(body, jnp.zeros((SUBLANES, qw), F32)), axis=0, keepdims=True)

    def bisect(_, carry):
        lo, hi, c_hi = carry
        mid = 0.5 * lo + 0.5 * hi
        cnt = count(lambda s: s >= mid)
        ge = cnt >= kf
        return jnp.where(ge, mid, lo), jnp.where(ge, hi, mid), jnp.where(ge, c_hi, cnt)

    hi0 = smax + (jnp.abs(smax) + 1.0) * (2.0 ** -10)
    _, hi, c_hi = lax.fori_loop(0, N_BISECT, bisect, (smin, hi0, jnp.zeros((1, qw), F32)))

    def snap(carry):
        hi, c_hi, done, tau, n_gt = carry

        def vbody(kbs, acc):
            for kb in kbs:
                s = s_ref[kb]
                acc = jnp.maximum(acc, _slab_reduce(jnp.where(s < hi, s, -BIG), jnp.max))
            return acc
        cand = jnp.max(over_blocks(vbody, jnp.full((SUBLANES, qw), -BIG, F32)), axis=0, keepdims=True)
        c_ge = count(lambda s: s >= cand)
        ok = c_ge >= kf
        fresh = ok & (done < 0.5)
        return (jnp.where(ok, hi, cand), jnp.where(ok, c_hi, c_ge), jnp.where(ok, 1.0, done),
                jnp.where(fresh, cand, tau), jnp.where(fresh, c_hi, n_gt))

    few = n_adm <= kf
    carry0 = (hi, c_hi, jnp.where(few, 1.0, 0.0), jnp.full((1, qw), 0.5 * NEG, F32), jnp.zeros((1, qw), F32))
    _, _, _, tau, n_gt = lax.while_loop(lambda c: jnp.min(c[2]) < 0.5, snap, carry0)
    need = kf - n_gt

    r_i = lax.broadcasted_iota(jnp.int32, (LANES, LANES), 0)
    c_i = lax.broadcasted_iota(jnp.int32, (LANES, LANES), 1)
    tri = jnp.where(c_i <= r_i, 1.0, 0.0).astype(BF16)

    def select_body(kbs, off):
        for kb in kbs:
            for r0 in range(0, kb_rows, LANES):
                s = s_ref[kb, r0:r0 + LANES, :]
                eq = s == tau
                rank = _dot(tri, jnp.where(eq, 1.0, 0.0).astype(BF16))
                y = jnp.where(eq, rank + off, jnp.where(s > tau, 0.0, BIG))
                s_ref[kb, r0:r0 + LANES, :] = jnp.where(y <= need, 0.0, NEG)
                off = off + rank[LANES - 1:LANES, :]
        return off

    over_blocks(select_body, jnp.zeros((1, qw), F32))

    pair_w = 2 * HEAD_DIM
    pr = lax.broadcasted_iota(jnp.int32, (pair_w, qw), 0)
    q_pairs = []
    for pair in range(N_HEADS // 2):
        qp = q_ref[0, pair * pair_w:(pair + 1) * pair_w, :]
        zero = jnp.zeros_like(qp)
        q_pairs.append(jnp.concatenate([jnp.where(pr < HEAD_DIM, qp, zero),
                                        jnp.where(pr >= HEAD_DIM, qp, zero)], axis=1))

    ot_ref[...] = jnp.zeros(ot_ref.shape, F32)
    head_rows = [slice(hd * HEAD_DIM, (hd + 1) * HEAD_DIM) for hd in range(N_HEADS)]

    def logit_body(kbs, m8s):
        m8s = list(m8s)
        for kb in kbs:
            bias = s_ref[kb]
            for pair in range(N_HEADS // 2):
                lg2 = _dot(k_blk(kb, slice(pair * pair_w, (pair + 1) * pair_w)), q_pairs[pair])
                for half in range(2):
                    hd = 2 * pair + half
                    lg = lg2[:, half * qw:(half + 1) * qw] + bias
                    lg_ref[hd, kb] = lg
                    m8s[hd] = jnp.maximum(m8s[hd], _slab_reduce(lg, jnp.max))
        return tuple(m8s)

    m8s = over_blocks(logit_body, tuple(jnp.full((SUBLANES, qw), -BIG, F32) for _ in range(N_HEADS)))
    ms = [jnp.max(m8, axis=0, keepdims=True) for m8 in m8s]

    def pv_body(kbs, l8s):
        l8s = list(l8s)
        for hd in range(N_HEADS):
            rows = head_rows[hd]
            acc = ot_ref[rows, :]
            for kb in kbs:
                e = jnp.exp2(lg_ref[hd, kb] - ms[hd])
                acc = acc + _dot(v_blk(kb, rows), e.astype(BF16))
                l8s[hd] = l8s[hd] + _slab_reduce(e, jnp.sum)
            ot_ref[rows, :] = acc
        return tuple(l8s)

    l8s = over_blocks(pv_body, tuple(jnp.zeros((SUBLANES, qw), F32) for _ in range(N_HEADS)))
    for hd in range(N_HEADS):
        rows = head_rows[hd]
        ot_ref[rows, :] = ot_ref[rows, :] / jnp.sum(l8s[hd], axis=0, keepdims=True)

    o_ref[0] = ot_ref[...].T.astype(BF16)


def _attend_call(qi_t, wi_t, q_t, keys, *, layer=None, qw, kb_rows, n_kb_max, n_kb_static,
                 q_pos0, n_valid, top_k):
    b, _, t_q = q_t.shape
    n_q = t_q // qw
    assemble = layer is not None
    qspec = lambda rows: pl.BlockSpec((1, rows, qw), lambda i, j: (i, 0, j))
    scratch = [pltpu.VMEM((n_kb_max, kb_rows, qw), F32), pltpu.VMEM((N_HEADS, n_kb_max, kb_rows, qw), F32),
               pltpu.VMEM((ATT_W, qw), F32)]
    if assemble:
        kic, kc, vtc, kin, kn, vtn = keys
        cache = lambda a: pl.BlockSpec((None, None) + a.shape[2:], lambda i, j: (layer, i, 0, 0))
        fresh = lambda a: pl.BlockSpec((None,) + a.shape[1:], lambda i, j: (i, 0, 0))
        args = [qi_t, wi_t, kic, kin, kc, kn, q_t, vtc, vtn]
        in_specs = [qspec(IDX_W), qspec(SUBLANES), cache(kic), fresh(kin), cache(kc), fresh(kn),
                    qspec(ATT_W), cache(vtc), fresh(vtn)]
        scratch += [pltpu.VMEM((1, kb_rows, IDX_DIM), BF16), pltpu.VMEM((1, kb_rows, ATT_W), BF16),
                    pltpu.VMEM((1, ATT_W, kb_rows), BF16)]
    else:
        ki_b, k_b, v_tb = keys
        batch_blk = lambda a: pl.BlockSpec((1,) + a.shape[1:], lambda i, j: (i, 0, 0, 0))
        args = [qi_t, wi_t, ki_b, k_b, q_t, v_tb]
        in_specs = [qspec(IDX_W), qspec(SUBLANES), batch_blk(ki_b), batch_blk(k_b), qspec(ATT_W),
                    batch_blk(v_tb)]
    kern = functools.partial(_attend_kernel, kb_rows=kb_rows, qw=qw, n_kb_static=n_kb_static,
                             q_pos0=q_pos0, n_valid=n_valid, top_k=top_k, assemble=assemble)
    return pl.pallas_call(
        kern,
        out_shape=jax.ShapeDtypeStruct((b, t_q, ATT_W), BF16),
        grid=(b, n_q),
        in_specs=in_specs,
        out_specs=pl.BlockSpec((1, qw, ATT_W), lambda i, j: (i, j, 0)),
        scratch_shapes=scratch,
        compiler_params=pltpu.CompilerParams(dimension_semantics=("arbitrary", "arbitrary"),
                                             vmem_limit_bytes=VMEM_LIMIT),
        name="attend_cached" if assemble else "attend",
    )(*args)


def _rope_tables(pos):
    half = HEAD_DIM // 2
    freqs = ROPE_THETA ** (-jnp.arange(half, dtype=F32) / half)
    ang = pos.astype(F32)[:, None] * freqs[None, :]
    cos, sin = jnp.cos(ang), jnp.sin(ang)
    cos_h = jnp.concatenate([cos, cos], axis=-1)
    sin_h = jnp.concatenate([-sin, sin], axis=-1)
    n = pos.shape[0]
    pad_c = jnp.ones((n, LANES - IDX_DIM), F32)
    pad_s = jnp.zeros((n, LANES - IDX_DIM), F32)
    return (jnp.tile(cos_h, (1, N_HEADS)), jnp.tile(sin_h, (1, N_HEADS)),
            jnp.concatenate([cos_h, pad_c], axis=-1), jnp.concatenate([sin_h, pad_s], axis=-1))


def _prep_weights(ffn1_norm, ffn1_up, ffn1_down, mix_norm, w_in, conv_w, w_branch, w_out,
                  ffn2_norm, ffn2_up, ffn2_down):
    d = w_in.shape[1]
    c = lambda a: a.astype(BF16)
    gain = lambda a: a[:, None, :]
    sizes = (3 * ATT_W, IDX_W, IDX_DIM + IDX_HEADS, 3 * CONV_W, d, d)
    offs = [0]
    for s in sizes:
        offs.append(offs[-1] + s)
    wqkv, wqi, wkw, wconv, wga, wgb = (w_in[:, :, offs[n]:offs[n + 1]] for n in range(len(sizes)))
    wkw = jnp.pad(wkw, ((0, 0), (0, 0), (0, LANES - IDX_DIM - IDX_HEADS)))
    return dict(
        ffn1_g=gain(ffn1_norm), ffn1_up=c(ffn1_up), ffn1_dn=c(ffn1_down),
        mix_g=gain(mix_norm), wqkv=c(wqkv), wqi=c(wqi), wkw=c(wkw), wconv=c(wconv),
        wga=c(wga), wgb=c(wgb), conv_w=conv_w,
        wb1=c(w_branch[:, :ATT_W]), wb2=c(w_branch[:, ATT_W:]), wout=c(w_out),
        ffn2_g=gain(ffn2_norm), ffn2_up=c(ffn2_up), ffn2_dn=c(ffn2_down),
    )


def _pick_tile(n, pref):
    t = min(pref, n)
    while n % t:
        t //= 2
    return t


def kernel(x_prompt, x_sample, cache_k, cache_v, cache_kidx, state_conv, ffn1_norm, ffn1_up, ffn1_down, mix_norm, w_in, conv_w, w_branch, w_out, ffn2_norm, ffn2_up, ffn2_down, final_norm):
    depth = w_in.shape[0]
    bp, tp, d = x_prompt.shape
    bs, ts, _ = x_sample.shape
    past = cache_k.shape[2]
    assert past % LANES == 0 and ts % (2 * SUBLANES) == 0 and ts <= LANES
    qw_p = 2 * LANES
    qw_s = LANES
    l_s = past + ts
    kb_s = -(-l_s // LANES) * LANES
    topk_p = min(TOPK_MAX, tp // 4)
    topk_s = min(TOPK_MAX, l_s // 4)

    w = _prep_weights(ffn1_norm, ffn1_up, ffn1_down, mix_norm, w_in, conv_w, w_branch, w_out,
                      ffn2_norm, ffn2_up, ffn2_down)
    tabs_p = _rope_tables(jnp.arange(tp))
    tabs_s = _rope_tables(past + jnp.arange(ts))
    fin = final_norm[None]
    zero_conv = jnp.zeros((bp, CONV_K - 1, CONV_W), F32)
    tm_p = _pick_tile(bp * tp, 512)
    tm_s = _pick_tile(bs * ts, 512)
    tm_proj = 2 * qw_p if tp % (2 * qw_p) == 0 else qw_p

    kc_all = jnp.swapaxes(cache_k.reshape(depth, bs, past, ATT_W), 2, 3)
    vtc_all = jnp.swapaxes(cache_v.reshape(depth, bs, past, ATT_W), 2, 3)
    kic_all = cache_kidx.astype(BF16)

    xp = x_prompt.reshape(bp * tp, d)
    xs = x_sample.reshape(bs * ts, d)
    outs = [[] for _ in range(8)]
    for i in range(depth):
        last = i == depth - 1

        xp = _ffn_call(xp, w, i, "ffn1", tm=tm_p)
        (kf, vf, kif, kb, kib, qt, vtb, qit, wit, gbyb, nconv) = _proj_call(
            xp.reshape(bp, tp, d), w, i, zero_conv, tabs_p, tm=tm_proj, kb_rows=qw_p)
        oatt = _attend_call(qit, wit, qt, (kib, kb, vtb), qw=qw_p, kb_rows=qw_p, n_kb_max=tp // qw_p,
                            n_kb_static=None, q_pos0=0, n_valid=tp, top_k=topk_p)
        xp = _ffn_call(xp, w, i, "ffn2", mix=(oatt.reshape(bp * tp, ATT_W), gbyb.reshape(bp * tp, d)),
                       final_g=fin if last else None, tm=tm_p)
        for lst, a in zip(outs[:4], (kf, vf, kif, nconv)):
            lst.append(a)

        xs = _ffn_call(xs, w, i, "ffn1", tm=tm_s)
        (kf, vf, kif, kb, kib, qt, vtb, qit, wit, gbyb, nconv) = _proj_call(
            xs.reshape(bs, ts, d), w, i, state_conv[i], tabs_s, tm=ts, kb_rows=ts)
        oatt = _attend_call(qit, wit, qt, (kic_all, kc_all, vtc_all, kib[:, 0], kb[:, 0], vtb[:, 0]),
                            layer=i, qw=qw_s, kb_rows=kb_s, n_kb_max=1, n_kb_static=1,
                            q_pos0=past, n_valid=l_s, top_k=topk_s)[:, :ts]
        xs = _ffn_call(xs, w, i, "ffn2", mix=(oatt.reshape(bs * ts, ATT_W), gbyb.reshape(bs * ts, d)),
                       final_g=fin if last else None, tm=tm_s)
        for lst, a in zip(outs[4:], (kf, vf, kif, nconv)):
            lst.append(a)

    heads = lambda a: a.reshape(a.shape[:-1] + (N_HEADS, HEAD_DIM))
    st = [jnp.stack(l) for l in outs]
    return (xp.reshape(bp, tp, d), xs.reshape(bs, ts, d),
            heads(st[0]), heads(st[1]), st[2], st[3],
            heads(st[4]), heads(st[5]), st[6], st[7])
```

```python
import functools
import math

import jax
import jax.numpy as jnp
from jax import lax
from jax.experimental import pallas as pl
from jax.experimental.pallas import tpu as pltpu

CHUNK = 64
N_HEADS = 8
HEAD_DIM = 64
ATT_W = N_HEADS * HEAD_DIM
IDX_HEADS = 4
IDX_DIM = 64
IDX_W = IDX_HEADS * IDX_DIM
TOPK_MAX = 256
CONV_W = 512
CONV_K = 3
ROPE_THETA = 10000.0
EPS = 1e-6
NEG = -1e30
BIG = 3e38
IDX_SCALE = (IDX_DIM ** -0.5) * (IDX_HEADS ** -0.5)
Q_SCALE = (HEAD_DIM ** -0.5) * math.log2(math.e)

LANES = 128
SUBLANES = 8
MXU_W = 256
VMEM_LIMIT = 56 * 1024 * 1024
N_BISECT = 18

BF16 = jnp.bfloat16
F32 = jnp.float32


def _dot(a, b):
    return jnp.dot(a, b, preferred_element_type=F32)


def _rms(x, g):
    return x * lax.rsqrt(jnp.mean(x * x, axis=-1, keepdims=True) + EPS) * g


def _const_spec(shape):
    nd = len(shape)
    return pl.BlockSpec(shape, lambda *_: (0,) * nd, pipeline_mode=pl.Buffered(1))


def _layer_spec(a, layer):
    nd = a.ndim
    return pl.BlockSpec((None,) + a.shape[1:], lambda *_: (layer,) + (0,) * (nd - 1),
                        pipeline_mode=pl.Buffered(1))


def _ffn_kernel(*refs, d_ff, chunks, has_mix, has_final):
    it = iter(refs)
    x_ref = next(it)
    if has_mix:
        oatt_ref, gbyb_ref, mixg_ref, wga_ref, wb1_ref, wout_ref = (next(it) for _ in range(6))
    g_ref, wup_ref, wdn_ref = next(it), next(it), next(it)
    fin_ref = next(it) if has_final else None
    o_ref = next(it)

    x = x_ref[...]
    if has_mix:
        hm = _rms(x, mixg_ref[...]).astype(BF16)
        g_a = jax.nn.sigmoid(_dot(hm, wga_ref[...]))
        y_a = _dot(oatt_ref[...], wb1_ref[...])
        mixed = g_a * y_a + gbyb_ref[...].astype(F32)
        x = x + _dot(mixed.astype(BF16), wout_ref[...])
    h = _rms(x, g_ref[...]).astype(BF16)
    acc = jnp.zeros(x.shape, F32)
    for c0, c1 in chunks:
        a = _dot(h, wup_ref[:, c0:c1])
        b = _dot(h, wup_ref[:, d_ff + c0:d_ff + c1])
        act = (a * jax.nn.sigmoid(a) * b).astype(BF16)
        acc = acc + _dot(act, wdn_ref[c0:c1, :])
    y = x + 0.5 * acc
    if has_final:
        y = _rms(y, fin_ref[...])
    o_ref[...] = y


def _ffn_call(x, w, layer, which, *, mix=None, final_g=None, tm):
    n, d = x.shape
    g, w_up, w_dn = w[which + "_g"], w[which + "_up"], w[which + "_dn"]
    d_ff = w_dn.shape[1]
    tiles = d_ff // MXU_W
    split = (tiles + 1) // 2 * MXU_W
    chunks = ((0, split), (split, d_ff)) if d_ff % MXU_W == 0 and tiles >= 2 else ((0, d_ff),)
    row = lambda width: pl.BlockSpec((tm, width), lambda i: (i, 0))
    args, specs = [x], [row(d)]
    if mix is not None:
        oatt, gbyb = mix
        consts = [w["mix_g"], w["wga"], w["wb1"], w["wout"]]
        args += [oatt, gbyb] + consts
        specs += [row(oatt.shape[1]), row(d)] + [_layer_spec(c, layer) for c in consts]
    args += [g, w_up, w_dn]
    specs += [_layer_spec(c, layer) for c in (g, w_up, w_dn)]
    if final_g is not None:
        args.append(final_g)
        specs.append(_const_spec(final_g.shape))
    kern = functools.partial(_ffn_kernel, d_ff=d_ff, chunks=chunks,
                             has_mix=mix is not None, has_final=final_g is not None)
    return pl.pallas_call(
        kern,
        out_shape=jax.ShapeDtypeStruct((n, d), F32),
        grid=(n // tm,),
        in_specs=specs,
        out_specs=row(d),
        compiler_params=pltpu.CompilerParams(dimension_semantics=("arbitrary",),
                                             vmem_limit_bytes=VMEM_LIMIT),
        name="ffn_mix" if mix is not None else "ffn",
    )(*args)


def _rope(y, c, s):
    n = y.shape[-1]
    lane = lax.broadcasted_iota(jnp.int32, y.shape, 1)
    first = (lane & (HEAD_DIM - 1)) < HEAD_DIM // 2
    rot = jnp.where(first, pltpu.roll(y, n - HEAD_DIM // 2, 1), pltpu.roll(y, HEAD_DIM // 2, 1))
    return y * c + rot * s


def _proj_kernel(*refs, tm, kb_rows, stacked):
    (x_ref, g_ref, wqkv_ref, wqi_ref, wkw_ref, wconv_ref, wgb_ref, wb2_ref,
     convw_ref, cprev_ref, cos_ref, sin_ref, coskw_ref, sinkw_ref) = refs[:14]
    (kf_ref, vf_ref, kif_ref, kb_ref, kib_ref, q_ref, v_ref, qi_ref, wi_ref,
     gbyb_ref, nconv_ref, ubuf) = refs[17 if stacked else 14:]
    t = pl.program_id(1)
    h = _rms(x_ref[0], g_ref[...]).astype(BF16)
    cosf, sinf = cos_ref[...], sin_ref[...]

    q = _rope(_dot(h, wqkv_ref[:, 0:ATT_W]), cosf, sinf) * Q_SCALE
    k = _rope(_dot(h, wqkv_ref[:, ATT_W:2 * ATT_W]), cosf, sinf)
    v = _dot(h, wqkv_ref[:, 2 * ATT_W:3 * ATT_W])
    qi = _rope(_dot(h, wqi_ref[...]), cosf[:, :IDX_W], sinf[:, :IDX_W])
    kw = _rope(_dot(h, wkw_ref[...]), coskw_ref[...], sinkw_ref[...])

    n_blk = tm // kb_rows
    kb_ref[0] = k.astype(BF16).reshape(n_blk, kb_rows, ATT_W)
    kib_ref[0] = kw[:, :IDX_DIM].astype(BF16).reshape(n_blk, kb_rows, IDX_DIM)

    def tokens_to_lanes(y):
        if tm < LANES:
            y = jnp.concatenate([y, jnp.zeros((LANES - tm, y.shape[1]), y.dtype)], axis=0)
        return y.T

    vt_f32 = tokens_to_lanes(v)
    kwt = tokens_to_lanes(kw)
    if stacked:
        kf_ref[...] = k.T
        vf_ref[...] = vt_f32
        kif_ref[...] = kwt[:IDX_DIM, :]
    else:
        kf_ref[0] = k
        vf_ref[0] = v
        kif_ref[0] = kw[:, :IDX_DIM]

    q_ref[0] = tokens_to_lanes(q).astype(BF16)
    vt = vt_f32.astype(BF16)
    if tm < LANES:
        v_ref[0, 0] = vt
    else:
        for n in range(n_blk):
            v_ref[0, n] = vt[:, n * kb_rows:(n + 1) * kb_rows]
    qi_ref[0] = tokens_to_lanes(qi).astype(BF16)
    wi_ref[0] = kwt[IDX_DIM:IDX_DIM + SUBLANES, :]

    cb = _dot(h, wconv_ref[:, 0:CONV_W])
    cc = _dot(h, wconv_ref[:, CONV_W:2 * CONV_W])
    cx = _dot(h, wconv_ref[:, 2 * CONV_W:3 * CONV_W])
    u = cc * cx

    @pl.when(t == 0)
    def _():
        ubuf[SUBLANES - 2:SUBLANES, :] = cprev_ref[0]

    ubuf[SUBLANES:SUBLANES + tm, :] = u
    um1 = ubuf[SUBLANES - 1:SUBLANES - 1 + tm, :]
    um2 = ubuf[SUBLANES - 2:SUBLANES - 2 + tm, :]
    cw = convw_ref[...]
    y_conv = cw[0:1, :] * um2 + cw[1:2, :] * um1 + cw[2:3, :] * u
    tail = u[tm - 2:tm, :]
    ubuf[SUBLANES - 2:SUBLANES, :] = tail
    nconv_ref[0] = tail

    y_b = _dot((cb * y_conv).astype(BF16), wb2_ref[...])
    g_b = jax.nn.sigmoid(_dot(h, wgb_ref[...]))
    gbyb_ref[0] = (g_b * y_b).astype(BF16)


def _proj_call(x, w, layer, conv_prev, tabs, *, tm, kb_rows, stacked=None):
    b, t, d = x.shape
    nt = t // tm
    cos, sin, coskw, sinkw = tabs
    consts = [w["mix_g"], w["wqkv"], w["wqi"], w["wkw"], w["wconv"], w["wgb"], w["wb2"], w["conv_w"]]
    tile3 = lambda width: pl.BlockSpec((1, tm, width), lambda i, j: (i, j, 0))
    n_blk = tm // kb_rows
    tile4 = lambda width: pl.BlockSpec((1, n_blk, kb_rows, width), lambda i, j: (i, j, 0, 0))
    tab = lambda a: pl.BlockSpec((tm, a.shape[1]), lambda i, j: (j, 0))
    in_specs = ([tile3(d)] + [_layer_spec(c, layer) for c in consts]
                + [pl.BlockSpec((1, CONV_K - 1, CONV_W), lambda i, j: (i, 0, 0))]
                + [tab(cos), tab(sin), tab(coskw), tab(sinkw)])
    args = [x, *consts, conv_prev, cos, sin, coskw, sinkw]
    aliases = {}
    if stacked is None:
        out_shape = [
            jax.ShapeDtypeStruct((b, t, ATT_W), F32),
            jax.ShapeDtypeStruct((b, t, ATT_W), F32),
            jax.ShapeDtypeStruct((b, t, IDX_DIM), F32),
        ]
        out_specs = [tile3(ATT_W), tile3(ATT_W), tile3(IDX_DIM)]
    else:
        out_shape = [jax.ShapeDtypeStruct(a.shape, a.dtype) for a in stacked]
        out_specs = [pl.BlockSpec((None, None, a.shape[2], tm), lambda i, j: (layer, i, 0, j)) for a in stacked]
        aliases = {len(args) + n: n for n in range(len(stacked))}
        in_specs += [pl.BlockSpec(memory_space=pl.ANY) for _ in stacked]
        args += list(stacked)
    out_shape += [
        jax.ShapeDtypeStruct((b, t // kb_rows, kb_rows, ATT_W), BF16),
        jax.ShapeDtypeStruct((b, t // kb_rows, kb_rows, IDX_DIM), BF16),
    ]
    out_specs += [tile4(ATT_W), tile4(IDX_DIM)]
    tl = max(tm, LANES)
    vt_blocks, vt_cols = ((nt, tl), (1, tl)) if tm < LANES else ((t // kb_rows, kb_rows), (n_blk, kb_rows))
    lanes3 = lambda rows: pl.BlockSpec((1, rows, tl), lambda i, j: (i, 0, j))
    out_shape += [
        jax.ShapeDtypeStruct((b, ATT_W, nt * tl), BF16),
        jax.ShapeDtypeStruct((b, vt_blocks[0], ATT_W, vt_blocks[1]), BF16),
        jax.ShapeDtypeStruct((b, IDX_W, nt * tl), BF16),
        jax.ShapeDtypeStruct((b, SUBLANES, nt * tl), F32),
    ]
    out_specs += [lanes3(ATT_W),
                  pl.BlockSpec((1, vt_cols[0], ATT_W, vt_cols[1]), lambda i, j: (i, j, 0, 0)),
                  lanes3(IDX_W), lanes3(SUBLANES)]
    out_shape += [
        jax.ShapeDtypeStruct((b, t, d), BF16),
        jax.ShapeDtypeStruct((b, CONV_K - 1, CONV_W), F32),
    ]
    out_specs += [tile3(d), pl.BlockSpec((1, CONV_K - 1, CONV_W), lambda i, j: (i, 0, 0))]
    kern = functools.partial(_proj_kernel, tm=tm, kb_rows=kb_rows, stacked=stacked is not None)
    return pl.pallas_call(
        kern,
        out_shape=out_shape,
        grid=(b, nt),
        in_specs=in_specs,
        out_specs=out_specs,
        scratch_shapes=[pltpu.VMEM((tm + SUBLANES, CONV_W), F32)],
        input_output_aliases=aliases,
        compiler_params=pltpu.CompilerParams(dimension_semantics=("arbitrary", "arbitrary"),
                                             vmem_limit_bytes=VMEM_LIMIT),
        name="proj",
    )(*args)


def _slab_reduce(x, op):
    kb, qw = x.shape
    return op(x.reshape(kb // SUBLANES, SUBLANES, qw), axis=0)


def _attend_kernel(*refs, kb_rows, qw, n_kb_static, q_pos0, n_valid, top_k, assemble):
    j = pl.program_id(1)
    n_kb = (j + 1) if n_kb_static is None else n_kb_static
    kf = float(top_k)

    if assemble:
        (qi_ref, wi_ref, kic_ref, kin_ref, kc_ref, kn_ref, q_ref, vtc_ref, vtn_ref,
         o_ref, s_ref, lg_ref, ot_ref, ki_s, k_s, v_s) = refs
        past, new = kc_ref.shape[1], kn_ref.shape[0]
        for dst, cached, fresh in ((ki_s, kic_ref[...], kin_ref), (k_s, kc_ref[...].T.astype(BF16), kn_ref)):
            dst[0, 0:past, :] = cached
            dst[0, past:past + new, :] = fresh[...]
            dst[0, past + new:, :] = jnp.zeros((kb_rows - past - new, dst.shape[2]), BF16)
        v_s[0, :, 0:past] = vtc_ref[...].astype(BF16)
        v_s[0, :, past:] = vtn_ref[...]
        ki_blk = lambda kb: ki_s[kb]
        k_blk = lambda kb, lanes: k_s[kb, :, lanes]
        v_blk = lambda kb, rows: v_s[kb, rows, :]
    else:
        qi_ref, wi_ref, ki_ref, k_ref, q_ref, v_ref, o_ref, s_ref, lg_ref, ot_ref = refs
        ki_blk = lambda kb: ki_ref[0, kb]
        k_blk = lambda kb, lanes: k_ref[0, kb, :, lanes]
        v_blk = lambda kb, rows: v_ref[0, kb, rows, :]

    def over_blocks(body, init, n=None):
        n = n_kb if n is None else n
        pairs = lax.fori_loop(0, n // 2, lambda i, c: body([2 * i, 2 * i + 1], c), init)
        if isinstance(n, int):
            return body([n - 1], pairs) if n % 2 else pairs
        return lax.cond(n % 2 == 1, lambda c: body([n - 1], c), lambda c: c, pairs)

    last = n_kb - 1
    row = lax.broadcasted_iota(jnp.int32, (kb_rows, qw), 0) + last * kb_rows
    qpos = lax.broadcasted_iota(jnp.int32, (kb_rows, qw), 1) + (q_pos0 + j * qw)
    chunk_shift = CHUNK.bit_length() - 1
    adm_last = ((row >> chunk_shift) <= (qpos >> chunk_shift)) & (row < n_valid)

    def score_block(kb):
        ki = ki_blk(kb)
        s = jnp.zeros((kb_rows, qw), F32)
        for hh in range(IDX_HEADS):
            d = _dot(ki, qi_ref[0, hh * IDX_DIM:(hh + 1) * IDX_DIM, :])
            s = s + jnp.maximum(d, 0.0) * wi_ref[0, hh:hh + 1, :]
        return s * IDX_SCALE

    def score_body(kbs, carry):
        mx, mn = carry
        for kb in kbs:
            s = score_block(kb)
            s_ref[kb] = s
            mx = jnp.maximum(mx, _slab_reduce(s, jnp.max))
            mn = jnp.minimum(mn, _slab_reduce(s, jnp.min))
        return mx, mn

    mx8 = jnp.full((SUBLANES, qw), -BIG, F32)
    mn8 = jnp.full((SUBLANES, qw), BIG, F32)
    mx8, mn8 = over_blocks(score_body, (mx8, mn8), n=last)
    s_last = score_block(last)
    s_ref[last] = jnp.where(adm_last, s_last, NEG)
    mx8 = jnp.maximum(mx8, _slab_reduce(jnp.where(adm_last, s_last, -BIG), jnp.max))
    mn8 = jnp.minimum(mn8, _slab_reduce(jnp.where(adm_last, s_last, BIG), jnp.min))
    smax = jnp.max(mx8, axis=0, keepdims=True)
    smin = jnp.min(mn8, axis=0, keepdims=True)
    n_adm = (jnp.sum(_slab_reduce(jnp.where(adm_last, 1.0, 0.0), jnp.sum), axis=0, keepdims=True)
             + jnp.float32(1.0) * (last * kb_rows))

    def count(pred_fn):
        def body(kbs, acc):
            for kb in kbs:
                acc = acc + _slab_reduce(jnp.where(pred_fn(s_ref[kb]), 1.0, 0.0), jnp.sum)
            return acc
        return jnp.sum(over_blocks(body, jnp.zeros((SUBLANES, qw), F32)), axis=0, keepdims=True)

    def bisect(_, carry):
        lo, hi, c_hi = carry
        mid = 0.5 * lo + 0.5 * hi
        cnt = count(lambda s: s >= mid)
        ge = cnt >= kf
        return jnp.where(ge, mid, lo), jnp.where(ge, hi, mid), jnp.where(ge, c_hi, cnt)

    hi0 = smax + (jnp.abs(smax) + 1.0) * (2.0 ** -10)
    _, hi, c_hi = lax.fori_loop(0, N_BISECT, bisect, (smin, hi0, jnp.zeros((1, qw), F32)))

    def snap(carry):
        hi, c_hi, done, tau, n_gt = carry

        def vbody(kbs, acc):
            for kb in kbs:
                s = s_ref[kb]
                acc = jnp.maximum(acc, _slab_reduce(jnp.where(s < hi, s, -BIG), jnp.max))
            return acc
        cand = jnp.max(over_blocks(vbody, jnp.full((SUBLANES, qw), -BIG, F32)), axis=0, keepdims=True)
        c_ge = count(lambda s: s >= cand)
        ok = c_ge >= kf
        fresh = ok & (done < 0.5)
        return (jnp.where(ok, hi, cand), jnp.where(ok, c_hi, c_ge), jnp.where(ok, 1.0, done),
                jnp.where(fresh, cand, tau), jnp.where(fresh, c_hi, n_gt))

    few = n_adm <= kf
    carry0 = (hi, c_hi, jnp.where(few, 1.0, 0.0), jnp.full((1, qw), 0.5 * NEG, F32), jnp.zeros((1, qw), F32))
    _, _, _, tau, n_gt = lax.while_loop(lambda c: jnp.min(c[2]) < 0.5, snap, carry0)
    need = kf - n_gt

    r_i = lax.broadcasted_iota(jnp.int32, (LANES, LANES), 0)
    c_i = lax.broadcasted_iota(jnp.int32, (LANES, LANES), 1)
    tri = jnp.where(c_i <= r_i, 1.0, 0.0).astype(BF16)

    def select_body(kbs, off):
        for kb in kbs:
            for r0 in range(0, kb_rows, LANES):
                s = s_ref[kb, r0:r0 + LANES, :]
                eq = s == tau
                rank = _dot(tri, jnp.where(eq, 1.0, 0.0).astype(BF16))
                y = jnp.where(eq, rank + off, jnp.where(s > tau, 0.0, BIG))
                s_ref[kb, r0:r0 + LANES, :] = jnp.where(y <= need, 0.0, NEG)
                off = off + rank[LANES - 1:LANES, :]
        return off

    over_blocks(select_body, jnp.zeros((1, qw), F32))

    pair_w = 2 * HEAD_DIM
    pr = lax.broadcasted_iota(jnp.int32, (pair_w, qw), 0)
    q_pairs = []
    for pair in range(N_HEADS // 2):
        qp = q_ref[0, pair * pair_w:(pair + 1) * pair_w, :]
        zero = jnp.zeros_like(qp)
        q_pairs.append(jnp.concatenate([jnp.where(pr < HEAD_DIM, qp, zero),
                                        jnp.where(pr >= HEAD_DIM, qp, zero)], axis=1))

    ot_ref[...] = jnp.zeros(ot_ref.shape, F32)
    head_rows = [slice(hd * HEAD_DIM, (hd + 1) * HEAD_DIM) for hd in range(N_HEADS)]

    def logit_body(kbs, m8s):
        m8s = list(m8s)
        for kb in kbs:
            bias = s_ref[kb]
            for pair in range(N_HEADS // 2):
                lg2 = _dot(k_blk(kb, slice(pair * pair_w, (pair + 1) * pair_w)), q_pairs[pair])
                for half in range(2):
                    hd = 2 * pair + half
                    lg = lg2[:, half * qw:(half + 1) * qw] + bias
                    lg_ref[hd, kb] = lg
                    m8s[hd] = jnp.maximum(m8s[hd], _slab_reduce(lg, jnp.max))
        return tuple(m8s)

    m8s = over_blocks(logit_body, tuple(jnp.full((SUBLANES, qw), -BIG, F32) for _ in range(N_HEADS)))
    ms = [jnp.max(m8, axis=0, keepdims=True) for m8 in m8s]

    def pv_body(kbs, l8s):
        l8s = list(l8s)
        for hd in range(N_HEADS):
            rows = head_rows[hd]
            acc = ot_ref[rows, :]
            for kb in kbs:
                e = jnp.exp2(lg_ref[hd, kb] - ms[hd])
                acc = acc + _dot(v_blk(kb, rows), e.astype(BF16))
                l8s[hd] = l8s[hd] + _slab_reduce(e, jnp.sum)
            ot_ref[rows, :] = acc
        return tuple(l8s)

    l8s = over_blocks(pv_body, tuple(jnp.zeros((SUBLANES, qw), F32) for _ in range(N_HEADS)))
    for hd in range(N_HEADS):
        rows = head_rows[hd]
        ot_ref[rows, :] = ot_ref[rows, :] / jnp.sum(l8s[hd], axis=0, keepdims=True)

    o_ref[0] = ot_ref[...].T.astype(BF16)


def _attend_call(qi_t, wi_t, q_t, keys, *, layer=None, qw, kb_rows, n_kb_max, n_kb_static,
                 q_pos0, n_valid, top_k):
    b, _, t_q = q_t.shape
    n_q = t_q // qw
    assemble = layer is not None
    qspec = lambda rows: pl.BlockSpec((1, rows, qw), lambda i, j: (i, 0, j))
    scratch = [pltpu.VMEM((n_kb_max, kb_rows, qw), F32), pltpu.VMEM((N_HEADS, n_kb_max, kb_rows, qw), F32),
               pltpu.VMEM((ATT_W, qw), F32)]
    if assemble:
        kic, kc, vtc, kin, kn, vtn = keys
        cache = lambda a: pl.BlockSpec((None, None) + a.shape[2:], lambda i, j: (layer, i, 0, 0))
        fresh = lambda a: pl.BlockSpec((None,) + a.shape[1:], lambda i, j: (i, 0, 0))
        args = [qi_t, wi_t, kic, kin, kc, kn, q_t, vtc, vtn]
        in_specs = [qspec(IDX_W), qspec(SUBLANES), cache(kic), fresh(kin), cache(kc), fresh(kn),
                    qspec(ATT_W), cache(vtc), fresh(vtn)]
        scratch += [pltpu.VMEM((1, kb_rows, IDX_DIM), BF16), pltpu.VMEM((1, kb_rows, ATT_W), BF16),
                    pltpu.VMEM((1, ATT_W, kb_rows), BF16)]
    else:
        ki_b, k_b, v_tb = keys
        batch_blk = lambda a: pl.BlockSpec((1,) + a.shape[1:], lambda i, j: (i, 0, 0, 0))
        args = [qi_t, wi_t, ki_b, k_b, q_t, v_tb]
        in_specs = [qspec(IDX_W), qspec(SUBLANES), batch_blk(ki_b), batch_blk(k_b), qspec(ATT_W),
                    batch_blk(v_tb)]
    kern = functools.partial(_attend_kernel, kb_rows=kb_rows, qw=qw, n_kb_static=n_kb_static,
                             q_pos0=q_pos0, n_valid=n_valid, top_k=top_k, assemble=assemble)
    return pl.pallas_call(
        kern,
        out_shape=jax.ShapeDtypeStruct((b, t_q, ATT_W), BF16),
        grid=(b, n_q),
        in_specs=in_specs,
        out_specs=pl.BlockSpec((1, qw, ATT_W), lambda i, j: (i, j, 0)),
        scratch_shapes=scratch,
        compiler_params=pltpu.CompilerParams(dimension_semantics=("arbitrary", "arbitrary"),
                                             vmem_limit_bytes=VMEM_LIMIT),
        name="attend_cached" if assemble else "attend",
    )(*args)


def _rope_tables(pos):
    half = HEAD_DIM // 2
    freqs = ROPE_THETA ** (-jnp.arange(half, dtype=F32) / half)
    ang = pos.astype(F32)[:, None] * freqs[None, :]
    cos, sin = jnp.cos(ang), jnp.sin(ang)
    cos_h = jnp.concatenate([cos, cos], axis=-1)
    sin_h = jnp.concatenate([-sin, sin], axis=-1)
    n = pos.shape[0]
    pad_c = jnp.ones((n, LANES - IDX_DIM), F32)
    pad_s = jnp.zeros((n, LANES - IDX_DIM), F32)
    return (jnp.tile(cos_h, (1, N_HEADS)), jnp.tile(sin_h, (1, N_HEADS)),
            jnp.concatenate([cos_h, pad_c], axis=-1), jnp.concatenate([sin_h, pad_s], axis=-1))


def _prep_weights(ffn1_norm, ffn1_up, ffn1_down, mix_norm, w_in, conv_w, w_branch, w_out,
                  ffn2_norm, ffn2_up, ffn2_down):
    d = w_in.shape[1]
    c = lambda a: a.astype(BF16)
    gain = lambda a: a[:, None, :]
    sizes = (3 * ATT_W, IDX_W, IDX_DIM + IDX_HEADS, 3 * CONV_W, d, d)
    offs = [0]
    for s in sizes:
        offs.append(offs[-1] + s)
    wqkv, wqi, wkw, wconv, wga, wgb = (w_in[:, :, offs[n]:offs[n + 1]] for n in range(len(sizes)))
    wkw = jnp.pad(wkw, ((0, 0), (0, 0), (0, LANES - IDX_DIM - IDX_HEADS)))
    return dict(
        ffn1_g=gain(ffn1_norm), ffn1_up=c(ffn1_up), ffn1_dn=c(ffn1_down),
        mix_g=gain(mix_norm), wqkv=c(wqkv), wqi=c(wqi), wkw=c(wkw), wconv=c(wconv),
        wga=c(wga), wgb=c(wgb), conv_w=conv_w,
        wb1=c(w_branch[:, :ATT_W]), wb2=c(w_branch[:, ATT_W:]), wout=c(w_out),
        ffn2_g=gain(ffn2_norm), ffn2_up=c(ffn2_up), ffn2_dn=c(ffn2_down),
    )


def _pick_tile(n, pref):
    t = min(pref, n)
    while n % t:
        t //= 2
    return t


def kernel(x_prompt, x_sample, cache_k, cache_v, cache_kidx, state_conv, ffn1_norm, ffn1_up, ffn1_down, mix_norm, w_in, conv_w, w_branch, w_out, ffn2_norm, ffn2_up, ffn2_down, final_norm):
    depth = w_in.shape[0]
    bp, tp, d = x_prompt.shape
    bs, ts, _ = x_sample.shape
    past = cache_k.shape[2]
    assert past % LANES == 0 and ts % (2 * SUBLANES) == 0 and ts <= LANES
    qw_p = 2 * LANES
    qw_s = LANES
    l_s = past + ts
    kb_s = -(-l_s // LANES) * LANES
    topk_p = min(TOPK_MAX, tp // 4)
    topk_s = min(TOPK_MAX, l_s // 4)

    w = _prep_weights(ffn1_norm, ffn1_up, ffn1_down, mix_norm, w_in, conv_w, w_branch, w_out,
                      ffn2_norm, ffn2_up, ffn2_down)
    tabs_p = _rope_tables(jnp.arange(tp))
    tabs_s = _rope_tables(past + jnp.arange(ts))
    fin = final_norm[None]
    zero_conv = jnp.zeros((bp, CONV_K - 1, CONV_W), F32)
    tm_p = _pick_tile(bp * tp, 512)
    tm_s = _pick_tile(bs * ts, 512)
    tm_proj = 2 * qw_p if tp % (2 * qw_p) == 0 else qw_p

    kc_all = jnp.swapaxes(cache_k.reshape(depth, bs, past, ATT_W), 2, 3)
    vtc_all = jnp.swapaxes(cache_v.reshape(depth, bs, past, ATT_W), 2, 3)
    kic_all = cache_kidx.astype(BF16)

    xp = x_prompt.reshape(bp * tp, d)
    xs = x_sample.reshape(bs * ts, d)
    kvi = tuple(lax.empty((depth, bp, width, tp), F32) for width in (ATT_W, ATT_W, IDX_DIM))
    outs = [[] for _ in range(5)]
    for i in range(depth):
        last = i == depth - 1

        xp = _ffn_call(xp, w, i, "ffn1", tm=tm_p)
        (*kvi, kb, kib, qt, vtb, qit, wit, gbyb, nconv) = _proj_call(
            xp.reshape(bp, tp, d), w, i, zero_conv, tabs_p, tm=tm_proj, kb_rows=qw_p, stacked=kvi)
        oatt = _attend_call(qit, wit, qt, (kib, kb, vtb), qw=qw_p, kb_rows=qw_p, n_kb_max=tp // qw_p,
                            n_kb_static=None, q_pos0=0, n_valid=tp, top_k=topk_p)
        xp = _ffn_call(xp, w, i, "ffn2", mix=(oatt.reshape(bp * tp, ATT_W), gbyb.reshape(bp * tp, d)),
                       final_g=fin if last else None, tm=tm_p)
        outs[0].append(nconv)

        xs = _ffn_call(xs, w, i, "ffn1", tm=tm_s)
        (kf, vf, kif, kb, kib, qt, vtb, qit, wit, gbyb, nconv) = _proj_call(
            xs.reshape(bs, ts, d), w, i, state_conv[i], tabs_s, tm=ts, kb_rows=ts)
        oatt = _attend_call(qit, wit, qt, (kic_all, kc_all, vtc_all, kib[:, 0], kb[:, 0], vtb[:, 0]),
                            layer=i, qw=qw_s, kb_rows=kb_s, n_kb_max=1, n_kb_static=1,
                            q_pos0=past, n_valid=l_s, top_k=topk_s)[:, :ts]
        xs = _ffn_call(xs, w, i, "ffn2", mix=(oatt.reshape(bs * ts, ATT_W), gbyb.reshape(bs * ts, d)),
                       final_g=fin if last else None, tm=tm_s)
        for lst, a in zip(outs[1:], (kf, vf, kif, nconv)):
            lst.append(a)

    heads = lambda a: a.reshape(a.shape[:-1] + (N_HEADS, HEAD_DIM))
    heads_t = lambda a: jnp.transpose(a.reshape(depth, bp, N_HEADS, HEAD_DIM, tp), (0, 1, 4, 2, 3))
    st = [jnp.stack(l) for l in outs]
    return (xp.reshape(bp, tp, d), xs.reshape(bs, ts, d),
            heads_t(kvi[0]), heads_t(kvi[1]), jnp.swapaxes(kvi[2], 2, 3), st[0],
            heads(st[1]), heads(st[2]), st[3], st[4])
```

```python
import functools
import math

import jax
import jax.numpy as jnp
from jax import lax
from jax.experimental import pallas as pl
from jax.experimental.pallas import tpu as pltpu

CHUNK = 64
N_HEADS = 8
HEAD_DIM = 64
ATT_W = N_HEADS * HEAD_DIM
IDX_HEADS = 4
IDX_DIM = 64
IDX_W = IDX_HEADS * IDX_DIM
TOPK_MAX = 256
CONV_W = 512
CONV_K = 3
ROPE_THETA = 10000.0
EPS = 1e-6
NEG = -1e30
BIG = 3e38
IDX_SCALE = (IDX_DIM ** -0.5) * (IDX_HEADS ** -0.5)
Q_SCALE = (HEAD_DIM ** -0.5) * math.log2(math.e)

LANES = 128
SUBLANES = 8
MXU_W = 256
VMEM_LIMIT = 56 * 1024 * 1024
N_BISECT = 18

BF16 = jnp.bfloat16
F32 = jnp.float32


def _dot(a, b):
    return jnp.dot(a, b, preferred_element_type=F32)


def _rms(x, g):
    return x * lax.rsqrt(jnp.mean(x * x, axis=-1, keepdims=True) + EPS) * g


def _const_spec(shape):
    nd = len(shape)
    return pl.BlockSpec(shape, lambda *_: (0,) * nd, pipeline_mode=pl.Buffered(1))


def _layer_spec(a, layer):
    nd = a.ndim
    return pl.BlockSpec((None,) + a.shape[1:], lambda *_: (layer,) + (0,) * (nd - 1),
                        pipeline_mode=pl.Buffered(1))


def _ffn_kernel(*refs, d_ff, chunks, has_mix, has_final, n_main):
    it = iter(refs)
    streams = []
    for _ in range(2):
        x_ref = next(it)
        streams.append((x_ref, (next(it), next(it)) if has_mix else None))
    if has_mix:
        mixg_ref, wga_ref, wb1_ref, wout_ref = (next(it) for _ in range(4))
    g_ref, wup_ref, wdn_ref = next(it), next(it), next(it)
    fin_ref = next(it) if has_final else None
    o_refs = (next(it), next(it))

    def tile(x_ref, mix_refs, o_ref):
        x = x_ref[...]
        if has_mix:
            oatt_ref, gbyb_ref = mix_refs
            hm = _rms(x, mixg_ref[...]).astype(BF16)
            g_a = jax.nn.sigmoid(_dot(hm, wga_ref[...]))
            y_a = _dot(oatt_ref[...], wb1_ref[...])
            mixed = g_a * y_a + gbyb_ref[...].astype(F32)
            x = x + _dot(mixed.astype(BF16), wout_ref[...])
        h = _rms(x, g_ref[...]).astype(BF16)
        acc = jnp.zeros(x.shape, F32)
        for c0, c1 in chunks:
            a = _dot(h, wup_ref[:, c0:c1])
            b = _dot(h, wup_ref[:, d_ff + c0:d_ff + c1])
            act = (a * jax.nn.sigmoid(a) * b).astype(BF16)
            acc = acc + _dot(act, wdn_ref[c0:c1, :])
        y = x + 0.5 * acc
        if has_final:
            y = _rms(y, fin_ref[...])
        o_ref[...] = y

    i = pl.program_id(0)

    @pl.when(i < n_main)
    def _():
        tile(*streams[0], o_refs[0])

    @pl.when(i == n_main)
    def _():
        tile(*streams[1], o_refs[1])


def _ffn_call(x, x_tail, w, layer, which, *, mix=None, mix_tail=None, final_g=None, tm):
    n, d = x.shape
    n_main = n // tm
    g, w_up, w_dn = w[which + "_g"], w[which + "_up"], w[which + "_dn"]
    d_ff = w_dn.shape[1]
    tiles = d_ff // MXU_W
    split = (tiles + 1) // 2 * MXU_W
    chunks = ((0, split), (split, d_ff)) if d_ff % MXU_W == 0 and tiles >= 2 else ((0, d_ff),)
    row = lambda width: pl.BlockSpec((tm, width), lambda i: (jnp.minimum(i, n_main - 1), 0))
    whole = lambda a: pl.BlockSpec(a.shape, lambda i: (0, 0))
    args, specs = [], []
    for xx, mm, spec in ((x, mix, lambda a: row(a.shape[1])), (x_tail, mix_tail, whole)):
        args.append(xx)
        specs.append(spec(xx))
        if mix is not None:
            args += list(mm)
            specs += [spec(a) for a in mm]
    if mix is not None:
        consts = [w["mix_g"], w["wga"], w["wb1"], w["wout"]]
        args += consts
        specs += [_layer_spec(c, layer) for c in consts]
    args += [g, w_up, w_dn]
    specs += [_layer_spec(c, layer) for c in (g, w_up, w_dn)]
    if final_g is not None:
        args.append(final_g)
        specs.append(_const_spec(final_g.shape))
    kern = functools.partial(_ffn_kernel, d_ff=d_ff, chunks=chunks, n_main=n_main,
                             has_mix=mix is not None, has_final=final_g is not None)
    return pl.pallas_call(
        kern,
        out_shape=[jax.ShapeDtypeStruct(x.shape, F32), jax.ShapeDtypeStruct(x_tail.shape, F32)],
        grid=(n_main + 1,),
        in_specs=specs,
        out_specs=[row(d), whole(x_tail)],
        compiler_params=pltpu.CompilerParams(dimension_semantics=("arbitrary",),
                                             vmem_limit_bytes=VMEM_LIMIT),
        name="ffn_mix" if mix is not None else "ffn",
    )(*args)


def _rope(y, c, s):
    n = y.shape[-1]
    lane = lax.broadcasted_iota(jnp.int32, y.shape, 1)
    first = (lane & (HEAD_DIM - 1)) < HEAD_DIM // 2
    rot = jnp.where(first, pltpu.roll(y, n - HEAD_DIM // 2, 1), pltpu.roll(y, HEAD_DIM // 2, 1))
    return y * c + rot * s


def _proj_kernel(*refs, tm, kb_rows, stacked):
    (x_ref, g_ref, wqkv_ref, wqi_ref, wkw_ref, wconv_ref, wgb_ref, wb2_ref,
     convw_ref, cprev_ref, cos_ref, sin_ref, coskw_ref, sinkw_ref) = refs[:14]
    (kf_ref, vf_ref, kif_ref, kb_ref, kib_ref, q_ref, v_ref, qi_ref, wi_ref,
     gbyb_ref, nconv_ref, ubuf) = refs[17 if stacked else 14:]
    t = pl.program_id(1)
    h = _rms(x_ref[0], g_ref[...]).astype(BF16)
    cosf, sinf = cos_ref[...], sin_ref[...]

    q = _rope(_dot(h, wqkv_ref[:, 0:ATT_W]), cosf, sinf) * Q_SCALE
    k = _rope(_dot(h, wqkv_ref[:, ATT_W:2 * ATT_W]), cosf, sinf)
    v = _dot(h, wqkv_ref[:, 2 * ATT_W:3 * ATT_W])
    qi = _rope(_dot(h, wqi_ref[...]), cosf[:, :IDX_W], sinf[:, :IDX_W])
    kw = _rope(_dot(h, wkw_ref[...]), coskw_ref[...], sinkw_ref[...])

    n_blk = tm // kb_rows
    kb_ref[0] = k.astype(BF16).reshape(n_blk, kb_rows, ATT_W)
    kib_ref[0] = kw[:, :IDX_DIM].astype(BF16).reshape(n_blk, kb_rows, IDX_DIM)

    def tokens_to_lanes(y):
        if tm < LANES:
            y = jnp.concatenate([y, jnp.zeros((LANES - tm, y.shape[1]), y.dtype)], axis=0)
        return y.T

    vt_f32 = tokens_to_lanes(v)
    kwt = tokens_to_lanes(kw)
    if stacked:
        kf_ref[...] = k.T
        vf_ref[...] = vt_f32
        kif_ref[...] = kwt[:IDX_DIM, :]
    else:
        kf_ref[0] = k
        vf_ref[0] = v
        kif_ref[0] = kw[:, :IDX_DIM]

    q_ref[0] = tokens_to_lanes(q).astype(BF16)
    vt = vt_f32.astype(BF16)
    if tm < LANES:
        v_ref[0, 0] = vt
    else:
        for n in range(n_blk):
            v_ref[0, n] = vt[:, n * kb_rows:(n + 1) * kb_rows]
    qi_ref[0] = tokens_to_lanes(qi).astype(BF16)
    wi_ref[0] = kwt[IDX_DIM:IDX_DIM + SUBLANES, :]

    cb = _dot(h, wconv_ref[:, 0:CONV_W])
    cc = _dot(h, wconv_ref[:, CONV_W:2 * CONV_W])
    cx = _dot(h, wconv_ref[:, 2 * CONV_W:3 * CONV_W])
    u = cc * cx

    @pl.when(t == 0)
    def _():
        ubuf[SUBLANES - 2:SUBLANES, :] = cprev_ref[0]

    ubuf[SUBLANES:SUBLANES + tm, :] = u
    um1 = ubuf[SUBLANES - 1:SUBLANES - 1 + tm, :]
    um2 = ubuf[SUBLANES - 2:SUBLANES - 2 + tm, :]
    cw = convw_ref[...]
    y_conv = cw[0:1, :] * um2 + cw[1:2, :] * um1 + cw[2:3, :] * u
    tail = u[tm - 2:tm, :]
    ubuf[SUBLANES - 2:SUBLANES, :] = tail
    nconv_ref[0] = tail

    y_b = _dot((cb * y_conv).astype(BF16), wb2_ref[...])
    g_b = jax.nn.sigmoid(_dot(h, wgb_ref[...]))
    gbyb_ref[0] = (g_b * y_b).astype(BF16)


def _proj_call(x, w, layer, conv_prev, tabs, *, tm, kb_rows, stacked=None):
    b, t, d = x.shape
    nt = t // tm
    cos, sin, coskw, sinkw = tabs
    consts = [w["mix_g"], w["wqkv"], w["wqi"], w["wkw"], w["wconv"], w["wgb"], w["wb2"], w["conv_w"]]
    tile3 = lambda width: pl.BlockSpec((1, tm, width), lambda i, j: (i, j, 0))
    n_blk = tm // kb_rows
    tile4 = lambda width: pl.BlockSpec((1, n_blk, kb_rows, width), lambda i, j: (i, j, 0, 0))
    tab = lambda a: pl.BlockSpec((tm, a.shape[1]), lambda i, j: (j, 0))
    in_specs = ([tile3(d)] + [_layer_spec(c, layer) for c in consts]
                + [pl.BlockSpec((1, CONV_K - 1, CONV_W), lambda i, j: (i, 0, 0))]
                + [tab(cos), tab(sin), tab(coskw), tab(sinkw)])
    args = [x, *consts, conv_prev, cos, sin, coskw, sinkw]
    aliases = {}
    if stacked is None:
        out_shape = [
            jax.ShapeDtypeStruct((b, t, ATT_W), F32),
            jax.ShapeDtypeStruct((b, t, ATT_W), F32),
            jax.ShapeDtypeStruct((b, t, IDX_DIM), F32),
        ]
        out_specs = [tile3(ATT_W), tile3(ATT_W), tile3(IDX_DIM)]
    else:
        out_shape = [jax.ShapeDtypeStruct(a.shape, a.dtype) for a in stacked]
        out_specs = [pl.BlockSpec((None, None, a.shape[2], tm), lambda i, j: (layer, i, 0, j)) for a in stacked]
        aliases = {len(args) + n: n for n in range(len(stacked))}
        in_specs += [pl.BlockSpec(memory_space=pl.ANY) for _ in stacked]
        args += list(stacked)
    out_shape += [
        jax.ShapeDtypeStruct((b, t // kb_rows, kb_rows, ATT_W), BF16),
        jax.ShapeDtypeStruct((b, t // kb_rows, kb_rows, IDX_DIM), BF16),
    ]
    out_specs += [tile4(ATT_W), tile4(IDX_DIM)]
    tl = max(tm, LANES)
    vt_blocks, vt_cols = ((nt, tl), (1, tl)) if tm < LANES else ((t // kb_rows, kb_rows), (n_blk, kb_rows))
    lanes3 = lambda rows: pl.BlockSpec((1, rows, tl), lambda i, j: (i, 0, j))
    out_shape += [
        jax.ShapeDtypeStruct((b, ATT_W, nt * tl), BF16),
        jax.ShapeDtypeStruct((b, vt_blocks[0], ATT_W, vt_blocks[1]), BF16),
        jax.ShapeDtypeStruct((b, IDX_W, nt * tl), BF16),
        jax.ShapeDtypeStruct((b, SUBLANES, nt * tl), F32),
    ]
    out_specs += [lanes3(ATT_W),
                  pl.BlockSpec((1, vt_cols[0], ATT_W, vt_cols[1]), lambda i, j: (i, j, 0, 0)),
                  lanes3(IDX_W), lanes3(SUBLANES)]
    out_shape += [
        jax.ShapeDtypeStruct((b, t, d), BF16),
        jax.ShapeDtypeStruct((b, CONV_K - 1, CONV_W), F32),
    ]
    out_specs += [tile3(d), pl.BlockSpec((1, CONV_K - 1, CONV_W), lambda i, j: (i, 0, 0))]
    kern = functools.partial(_proj_kernel, tm=tm, kb_rows=kb_rows, stacked=stacked is not None)
    return pl.pallas_call(
        kern,
        out_shape=out_shape,
        grid=(b, nt),
        in_specs=in_specs,
        out_specs=out_specs,
        scratch_shapes=[pltpu.VMEM((tm + SUBLANES, CONV_W), F32)],
        input_output_aliases=aliases,
        compiler_params=pltpu.CompilerParams(dimension_semantics=("arbitrary", "arbitrary"),
                                             vmem_limit_bytes=VMEM_LIMIT),
        name="proj",
    )(*args)


def _slab_reduce(x, op):
    kb, qw = x.shape
    return op(x.reshape(kb // SUBLANES, SUBLANES, qw), axis=0)


def _attend_kernel(*refs, kb_rows, qw, n_kb_static, q_pos0, n_valid, top_k, assemble):
    j = pl.program_id(1)
    n_kb = (j + 1) if n_kb_static is None else n_kb_static
    kf = float(top_k)

    if assemble:
        (qi_ref, wi_ref, kic_ref, kin_ref, kc_ref, kn_ref, q_ref, vtc_ref, vtn_ref,
         o_ref, s_ref, lg_ref, ot_ref, ki_s, k_s, v_s) = refs
        past, new = kc_ref.shape[1], kn_ref.shape[0]
        for dst, cached, fresh in ((ki_s, kic_ref[...], kin_ref), (k_s, kc_ref[...].T.astype(BF16), kn_ref)):
            dst[0, 0:past, :] = cached
            dst[0, past:past + new, :] = fresh[...]
            dst[0, past + new:, :] = jnp.zeros((kb_rows - past - new, dst.shape[2]), BF16)
        v_s[0, :, 0:past] = vtc_ref[...].astype(BF16)
        v_s[0, :, past:] = vtn_ref[...]
        ki_blk = lambda kb: ki_s[kb]
        k_blk = lambda kb, lanes: k_s[kb, :, lanes]
        v_blk = lambda kb, rows: v_s[kb, rows, :]
    else:
        qi_ref, wi_ref, ki_ref, k_ref, q_ref, v_ref, o_ref, s_ref, lg_ref, ot_ref = refs
        ki_blk = lambda kb: ki_ref[0, kb]
        k_blk = lambda kb, lanes: k_ref[0, kb, :, lanes]
        v_blk = lambda kb, rows: v_ref[0, kb, rows, :]

    def over_blocks(body, init, n=None):
        n = n_kb if n is None else n
        pairs = lax.fori_loop(0, n // 2, lambda i, c: body([2 * i, 2 * i + 1], c), init)
        if isinstance(n, int):
            return body([n - 1], pairs) if n % 2 else pairs
        return lax.cond(n % 2 == 1, lambda c: body([n - 1], c), lambda c: c, pairs)

    last = n_kb - 1
    row = lax.broadcasted_iota(jnp.int32, (kb_rows, qw), 0) + last * kb_rows
    qpos = lax.broadcasted_iota(jnp.int32, (kb_rows, qw), 1) + (q_pos0 + j * qw)
    chunk_shift = CHUNK.bit_length() - 1
    adm_last = ((row >> chunk_shift) <= (qpos >> chunk_shift)) & (row < n_valid)

    def score_block(kb):
        ki = ki_blk(kb)
        s = jnp.zeros((kb_rows, qw), F32)
        for hh in range(IDX_HEADS):
            d = _dot(ki, qi_ref[0, hh * IDX_DIM:(hh + 1) * IDX_DIM, :])
            s = s + jnp.maximum(d, 0.0) * wi_ref[0, hh:hh + 1, :]
        return s * IDX_SCALE

    def score_body(kbs, carry):
        mx, mn = carry
        for kb in kbs:
            s = score_block(kb)
            s_ref[kb] = s
            mx = jnp.maximum(mx, _slab_reduce(s, jnp.max))
            mn = jnp.minimum(mn, _slab_reduce(s, jnp.min))
        return mx, mn

    mx8 = jnp.full((SUBLANES, qw), -BIG, F32)
    mn8 = jnp.full((SUBLANES, qw), BIG, F32)
    mx8, mn8 = over_blocks(score_body, (mx8, mn8), n=last)
    s_last = score_block(last)
    s_ref[last] = jnp.where(adm_last, s_last, NEG)
    mx8 = jnp.maximum(mx8, _slab_reduce(jnp.where(adm_last, s_last, -BIG), jnp.max))
    mn8 = jnp.minimum(mn8, _slab_reduce(jnp.where(adm_last, s_last, BIG), jnp.min))
    smax = jnp.max(mx8, axis=0, keepdims=True)
    smin = jnp.min(mn8, axis=0, keepdims=True)
    n_adm = (jnp.sum(_slab_reduce(jnp.where(adm_last, 1.0, 0.0), jnp.sum), axis=0, keepdims=True)
             + jnp.float32(1.0) * (last * kb_rows))

    def count(pred_fn):
        def body(kbs, acc):
            for kb in kbs:
                acc = acc + _slab_reduce(jnp.where(pred_fn(s_ref[kb]), 1.0, 0.0), jnp.sum)
            return acc
        return jnp.sum(over_blocks(body, jnp.zeros((SUBLANES, qw), F32)), axis=0, keepdims=True)

    def bisect(_, carry):
        lo, hi, c_hi = carry
        mid = 0.5 * lo + 0.5 * hi
        cnt = count(lambda s: s >= mid)
        ge = cnt >= kf
        return jnp.where(ge, mid, lo), jnp.where(ge, hi, mid), jnp.where(ge, c_hi, cnt)

    hi0 = smax + (jnp.abs(smax) + 1.0) * (2.0 ** -10)
    _, hi, c_hi = lax.fori_loop(0, N_BISECT, bisect, (smin, hi0, jnp.zeros((1, qw), F32)))

    def snap(carry):
        hi, c_hi, done, tau, n_gt = carry

        def vbody(kbs, acc):
            for kb in kbs:
                s = s_ref[kb]
                acc = jnp.maximum(acc, _slab_reduce(jnp.where(s < hi, s, -BIG), jnp.max))
            return acc
        cand = jnp.max(over_blocks(vbody, jnp.full((SUBLANES, qw), -BIG, F32)), axis=0, keepdims=True)
        c_ge = count(lambda s: s >= cand)
        ok = c_ge >= kf
        fresh = ok & (done < 0.5)
        return (jnp.where(ok, hi, cand), jnp.where(ok, c_hi, c_ge), jnp.where(ok, 1.0, done),
                jnp.where(fresh, cand, tau), jnp.where(fresh, c_hi, n_gt))

    few = n_adm <= kf
    carry0 = (hi, c_hi, jnp.where(few, 1.0, 0.0), jnp.full((1, qw), 0.5 * NEG, F32), jnp.zeros((1, qw), F32))
    _, _, _, tau, n_gt = lax.while_loop(lambda c: jnp.min(c[2]) < 0.5, snap, carry0)
    need = kf - n_gt

    r_i = lax.broadcasted_iota(jnp.int32, (LANES, LANES), 0)
    c_i = lax.broadcasted_iota(jnp.int32, (LANES, LANES), 1)
    tri = jnp.where(c_i <= r_i, 1.0, 0.0).astype(BF16)

    def select_body(kbs, off):
        for kb in kbs:
            for r0 in range(0, kb_rows, LANES):
                s = s_ref[kb, r0:r0 + LANES, :]
                eq = s == tau
                rank = _dot(tri, jnp.where(eq, 1.0, 0.0).astype(BF16))
                y = jnp.where(eq, rank + off, jnp.where(s > tau, 0.0, BIG))
                s_ref[kb, r0:r0 + LANES, :] = jnp.where(y <= need, 0.0, NEG)
                off = off + rank[LANES - 1:LANES, :]
        return off

    over_blocks(select_body, jnp.zeros((1, qw), F32))

    pair_w = 2 * HEAD_DIM
    pr = lax.broadcasted_iota(jnp.int32, (pair_w, qw), 0)
    q_pairs = []
    for pair in range(N_HEADS // 2):
        qp = q_ref[0, pair * pair_w:(pair + 1) * pair_w, :]
        zero = jnp.zeros_like(qp)
        q_pairs.append(jnp.concatenate([jnp.where(pr < HEAD_DIM, qp, zero),
                                        jnp.where(pr >= HEAD_DIM, qp, zero)], axis=1))

    ot_ref[...] = jnp.zeros(ot_ref.shape, F32)
    head_rows = [slice(hd * HEAD_DIM, (hd + 1) * HEAD_DIM) for hd in range(N_HEADS)]

    def logit_body(kbs, m8s):
        m8s = list(m8s)
        for kb in kbs:
            bias = s_ref[kb]
            for pair in range(N_HEADS // 2):
                lg2 = _dot(k_blk(kb, slice(pair * pair_w, (pair + 1) * pair_w)), q_pairs[pair])
                for half in range(2):
                    hd = 2 * pair + half
                    lg = lg2[:, half * qw:(half + 1) * qw] + bias
                    lg_ref[hd, kb] = lg
                    m8s[hd] = jnp.maximum(m8s[hd], _slab_reduce(lg, jnp.max))
        return tuple(m8s)

    m8s = over_blocks(logit_body, tuple(jnp.full((SUBLANES, qw), -BIG, F32) for _ in range(N_HEADS)))
    ms = [jnp.max(m8, axis=0, keepdims=True) for m8 in m8s]

    def pv_body(kbs, l8s):
        l8s = list(l8s)
        for hd in range(N_HEADS):
            rows = head_rows[hd]
            acc = ot_ref[rows, :]
            for kb in kbs:
                e = jnp.exp2(lg_ref[hd, kb] - ms[hd])
                acc = acc + _dot(v_blk(kb, rows), e.astype(BF16))
                l8s[hd] = l8s[hd] + _slab_reduce(e, jnp.sum)
            ot_ref[rows, :] = acc
        return tuple(l8s)

    l8s = over_blocks(pv_body, tuple(jnp.zeros((SUBLANES, qw), F32) for _ in range(N_HEADS)))
    for hd in range(N_HEADS):
        rows = head_rows[hd]
        ot_ref[rows, :] = ot_ref[rows, :] / jnp.sum(l8s[hd], axis=0, keepdims=True)

    o_ref[0] = ot_ref[...].T.astype(BF16)


def _attend_call(qi_t, wi_t, q_t, keys, *, layer=None, qw, kb_rows, n_kb_max, n_kb_static,
                 q_pos0, n_valid, top_k):
    b, _, t_q = q_t.shape
    n_q = t_q // qw
    assemble = layer is not None
    qspec = lambda rows: pl.BlockSpec((1, rows, qw), lambda i, j: (i, 0, j))
    scratch = [pltpu.VMEM((n_kb_max, kb_rows, qw), F32), pltpu.VMEM((N_HEADS, n_kb_max, kb_rows, qw), F32),
               pltpu.VMEM((ATT_W, qw), F32)]
    if assemble:
        kic, kc, vtc, kin, kn, vtn = keys
        cache = lambda a: pl.BlockSpec((None, None) + a.shape[2:], lambda i, j: (layer, i, 0, 0))
        fresh = lambda a: pl.BlockSpec((None,) + a.shape[1:], lambda i, j: (i, 0, 0))
        args = [qi_t, wi_t, kic, kin, kc, kn, q_t, vtc, vtn]
        in_specs = [qspec(IDX_W), qspec(SUBLANES), cache(kic), fresh(kin), cache(kc), fresh(kn),
                    qspec(ATT_W), cache(vtc), fresh(vtn)]
        scratch += [pltpu.VMEM((1, kb_rows, IDX_DIM), BF16), pltpu.VMEM((1, kb_rows, ATT_W), BF16),
                    pltpu.VMEM((1, ATT_W, kb_rows), BF16)]
    else:
        ki_b, k_b, v_tb = keys
        batch_blk = lambda a: pl.BlockSpec((1,) + a.shape[1:], lambda i, j: (i, 0, 0, 0))
        args = [qi_t, wi_t, ki_b, k_b, q_t, v_tb]
        in_specs = [qspec(IDX_W), qspec(SUBLANES), batch_blk(ki_b), batch_blk(k_b), qspec(ATT_W),
                    batch_blk(v_tb)]
    kern = functools.partial(_attend_kernel, kb_rows=kb_rows, qw=qw, n_kb_static=n_kb_static,
                             q_pos0=q_pos0, n_valid=n_valid, top_k=top_k, assemble=assemble)
    return pl.pallas_call(
        kern,
        out_shape=jax.ShapeDtypeStruct((b, t_q, ATT_W), BF16),
        grid=(b, n_q),
        in_specs=in_specs,
        out_specs=pl.BlockSpec((1, qw, ATT_W), lambda i, j: (i, j, 0)),
        scratch_shapes=scratch,
        compiler_params=pltpu.CompilerParams(dimension_semantics=("arbitrary", "arbitrary"),
                                             vmem_limit_bytes=VMEM_LIMIT),
        name="attend_cached" if assemble else "attend",
    )(*args)


def _rope_tables(pos):
    half = HEAD_DIM // 2
    freqs = ROPE_THETA ** (-jnp.arange(half, dtype=F32) / half)
    ang = pos.astype(F32)[:, None] * freqs[None, :]
    cos, sin = jnp.cos(ang), jnp.sin(ang)
    cos_h = jnp.concatenate([cos, cos], axis=-1)
    sin_h = jnp.concatenate([-sin, sin], axis=-1)
    n = pos.shape[0]
    pad_c = jnp.ones((n, LANES - IDX_DIM), F32)
    pad_s = jnp.zeros((n, LANES - IDX_DIM), F32)
    return (jnp.tile(cos_h, (1, N_HEADS)), jnp.tile(sin_h, (1, N_HEADS)),
            jnp.concatenate([cos_h, pad_c], axis=-1), jnp.concatenate([sin_h, pad_s], axis=-1))


def _prep_weights(ffn1_norm, ffn1_up, ffn1_down, mix_norm, w_in, conv_w, w_branch, w_out,
                  ffn2_norm, ffn2_up, ffn2_down):
    d = w_in.shape[1]
    c = lambda a: a.astype(BF16)
    gain = lambda a: a[:, None, :]
    sizes = (3 * ATT_W, IDX_W, IDX_DIM + IDX_HEADS, 3 * CONV_W, d, d)
    offs = [0]
    for s in sizes:
        offs.append(offs[-1] + s)
    wqkv, wqi, wkw, wconv, wga, wgb = (w_in[:, :, offs[n]:offs[n + 1]] for n in range(len(sizes)))
    wkw = jnp.pad(wkw, ((0, 0), (0, 0), (0, LANES - IDX_DIM - IDX_HEADS)))
    return dict(
        ffn1_g=gain(ffn1_norm), ffn1_up=c(ffn1_up), ffn1_dn=c(ffn1_down),
        mix_g=gain(mix_norm), wqkv=c(wqkv), wqi=c(wqi), wkw=c(wkw), wconv=c(wconv),
        wga=c(wga), wgb=c(wgb), conv_w=conv_w,
        wb1=c(w_branch[:, :ATT_W]), wb2=c(w_branch[:, ATT_W:]), wout=c(w_out),
        ffn2_g=gain(ffn2_norm), ffn2_up=c(ffn2_up), ffn2_dn=c(ffn2_down),
    )


def _pick_tile(n, pref):
    t = min(pref, n)
    while n % t:
        t //= 2
    return t


def kernel(x_prompt, x_sample, cache_k, cache_v, cache_kidx, state_conv, ffn1_norm, ffn1_up, ffn1_down, mix_norm, w_in, conv_w, w_branch, w_out, ffn2_norm, ffn2_up, ffn2_down, final_norm):
    depth = w_in.shape[0]
    bp, tp, d = x_prompt.shape
    bs, ts, _ = x_sample.shape
    past = cache_k.shape[2]
    assert past % LANES == 0 and ts % (2 * SUBLANES) == 0 and ts <= LANES
    qw_p = 2 * LANES
    qw_s = LANES
    l_s = past + ts
    kb_s = -(-l_s // LANES) * LANES
    topk_p = min(TOPK_MAX, tp // 4)
    topk_s = min(TOPK_MAX, l_s // 4)

    w = _prep_weights(ffn1_norm, ffn1_up, ffn1_down, mix_norm, w_in, conv_w, w_branch, w_out,
                      ffn2_norm, ffn2_up, ffn2_down)
    tabs_p = _rope_tables(jnp.arange(tp))
    tabs_s = _rope_tables(past + jnp.arange(ts))
    fin = final_norm[None]
    zero_conv = jnp.zeros((bp, CONV_K - 1, CONV_W), F32)
    tm_p = _pick_tile(bp * tp, 512)
    tm_proj = 2 * qw_p if tp % (2 * qw_p) == 0 else qw_p

    kc_all = jnp.swapaxes(cache_k.reshape(depth, bs, past, ATT_W), 2, 3)
    vtc_all = jnp.swapaxes(cache_v.reshape(depth, bs, past, ATT_W), 2, 3)
    kic_all = cache_kidx.astype(BF16)

    xp = x_prompt.reshape(bp * tp, d)
    xs = x_sample.reshape(bs * ts, d)
    kvi = tuple(lax.empty((depth, bp, width, tp), F32) for width in (ATT_W, ATT_W, IDX_DIM))
    outs = [[] for _ in range(5)]
    for i in range(depth):
        last = i == depth - 1

        xp, xs = _ffn_call(xp, xs, w, i, "ffn1", tm=tm_p)

        (*kvi, kb, kib, qt, vtb, qit, wit, gbyb_p, nconv) = _proj_call(
            xp.reshape(bp, tp, d), w, i, zero_conv, tabs_p, tm=tm_proj, kb_rows=qw_p, stacked=kvi)
        oatt_p = _attend_call(qit, wit, qt, (kib, kb, vtb), qw=qw_p, kb_rows=qw_p, n_kb_max=tp // qw_p,
                              n_kb_static=None, q_pos0=0, n_valid=tp, top_k=topk_p)
        outs[0].append(nconv)

        (kf, vf, kif, kb, kib, qt, vtb, qit, wit, gbyb_s, nconv) = _proj_call(
            xs.reshape(bs, ts, d), w, i, state_conv[i], tabs_s, tm=ts, kb_rows=ts)
        oatt_s = _attend_call(qit, wit, qt, (kic_all, kc_all, vtc_all, kib[:, 0], kb[:, 0], vtb[:, 0]),
                              layer=i, qw=qw_s, kb_rows=kb_s, n_kb_max=1, n_kb_static=1,
                              q_pos0=past, n_valid=l_s, top_k=topk_s)[:, :ts]
        for lst, a in zip(outs[1:], (kf, vf, kif, nconv)):
            lst.append(a)

        xp, xs = _ffn_call(xp, xs, w, i, "ffn2",
                           mix=(oatt_p.reshape(bp * tp, ATT_W), gbyb_p.reshape(bp * tp, d)),
                           mix_tail=(oatt_s.reshape(bs * ts, ATT_W), gbyb_s.reshape(bs * ts, d)),
                           final_g=fin if last else None, tm=tm_p)

    heads = lambda a: a.reshape(a.shape[:-1] + (N_HEADS, HEAD_DIM))
    heads_t = lambda a: jnp.transpose(a.reshape(depth, bp, N_HEADS, HEAD_DIM, tp), (0, 1, 4, 2, 3))
    st = [jnp.stack(l) for l in outs]
    return (xp.reshape(bp, tp, d), xs.reshape(bs, ts, d),
            heads_t(kvi[0]), heads_t(kvi[1]), jnp.swapaxes(kvi[2], 2, 3), st[0],
            heads(st[1]), heads(st[2]), st[3], st[4])
```

```python
import functools
import math

import jax
import jax.numpy as jnp
from jax import lax
from jax.experimental import pallas as pl
from jax.experimental.pallas import tpu as pltpu

CHUNK = 64
N_HEADS = 8
HEAD_DIM = 64
ATT_W = N_HEADS * HEAD_DIM
IDX_HEADS = 4
IDX_DIM = 64
IDX_W = IDX_HEADS * IDX_DIM
TOPK_MAX = 256
CONV_W = 512
CONV_K = 3
ROPE_THETA = 10000.0
EPS = 1e-6
NEG = -1e30
BIG = 3e38
IDX_SCALE = (IDX_DIM ** -0.5) * (IDX_HEADS ** -0.5)
Q_SCALE = (HEAD_DIM ** -0.5) * math.log2(math.e)

LANES = 128
SUBLANES = 8
MXU_W = 256
VMEM_LIMIT = 56 * 1024 * 1024
N_BISECT = 18

BF16 = jnp.bfloat16
F32 = jnp.float32


def _dot(a, b):
    return jnp.dot(a, b, preferred_element_type=F32)


def _rms(x, g):
    return x * lax.rsqrt(jnp.mean(x * x, axis=-1, keepdims=True) + EPS) * g


def _const_spec(shape):
    nd = len(shape)
    return pl.BlockSpec(shape, lambda *_: (0,) * nd, pipeline_mode=pl.Buffered(1))


def _layer_spec(a, layer):
    nd = a.ndim
    return pl.BlockSpec((None,) + a.shape[1:], lambda *_: (layer,) + (0,) * (nd - 1),
                        pipeline_mode=pl.Buffered(1))


def _ffn_kernel(*refs, d_ff, chunks, has_mix, has_final, n_main):
    it = iter(refs)
    streams = []
    for _ in range(2):
        x_ref = next(it)
        streams.append((x_ref, (next(it), next(it)) if has_mix else None))
    if has_mix:
        mixg_ref, wga_ref, wb1_ref, wout_ref = (next(it) for _ in range(4))
    g_ref, wup_ref, wdn_ref = next(it), next(it), next(it)
    fin_ref = next(it) if has_final else None
    o_refs = (next(it), next(it))

    def tile(x_ref, mix_refs, o_ref):
        x = x_ref[...]
        if has_mix:
            oatt_ref, gbyb_ref = mix_refs
            hm = _rms(x, mixg_ref[...]).astype(BF16)
            g_a = jax.nn.sigmoid(_dot(hm, wga_ref[...]))
            y_a = _dot(oatt_ref[...], wb1_ref[...])
            mixed = g_a * y_a + gbyb_ref[...].astype(F32)
            x = x + _dot(mixed.astype(BF16), wout_ref[...])
        h = _rms(x, g_ref[...]).astype(BF16)
        acc = jnp.zeros(x.shape, F32)
        for c0, c1 in chunks:
            a = _dot(h, wup_ref[:, c0:c1])
            b = _dot(h, wup_ref[:, d_ff + c0:d_ff + c1])
            act = (a * jax.nn.sigmoid(a) * b).astype(BF16)
            acc = acc + _dot(act, wdn_ref[c0:c1, :])
        y = x + 0.5 * acc
        if has_final:
            y = _rms(y, fin_ref[...])
        o_ref[...] = y

    i = pl.program_id(0)

    @pl.when(i < n_main)
    def _():
        tile(*streams[0], o_refs[0])

    @pl.when(i == n_main)
    def _():
        tile(*streams[1], o_refs[1])


def _ffn_call(x, x_tail, w, layer, which, *, mix=None, mix_tail=None, final_g=None, tm):
    n, d = x.shape
    n_main = n // tm
    g, w_up, w_dn = w[which + "_g"], w[which + "_up"], w[which + "_dn"]
    d_ff = w_dn.shape[1]
    tiles = d_ff // MXU_W
    split = (tiles + 1) // 2 * MXU_W
    chunks = ((0, split), (split, d_ff)) if d_ff % MXU_W == 0 and tiles >= 2 else ((0, d_ff),)
    row = lambda width: pl.BlockSpec((tm, width), lambda i: (jnp.minimum(i, n_main - 1), 0))
    whole = lambda a: pl.BlockSpec(a.shape, lambda i: (0, 0))
    args, specs = [], []
    for xx, mm, spec in ((x, mix, lambda a: row(a.shape[1])), (x_tail, mix_tail, whole)):
        args.append(xx)
        specs.append(spec(xx))
        if mix is not None:
            args += list(mm)
            specs += [spec(a) for a in mm]
    if mix is not None:
        consts = [w["mix_g"], w["wga"], w["wb1"], w["wout"]]
        args += consts
        specs += [_layer_spec(c, layer) for c in consts]
    args += [g, w_up, w_dn]
    specs += [_layer_spec(c, layer) for c in (g, w_up, w_dn)]
    if final_g is not None:
        args.append(final_g)
        specs.append(_const_spec(final_g.shape))
    kern = functools.partial(_ffn_kernel, d_ff=d_ff, chunks=chunks, n_main=n_main,
                             has_mix=mix is not None, has_final=final_g is not None)
    return pl.pallas_call(
        kern,
        out_shape=[jax.ShapeDtypeStruct(x.shape, F32), jax.ShapeDtypeStruct(x_tail.shape, F32)],
        grid=(n_main + 1,),
        in_specs=specs,
        out_specs=[row(d), whole(x_tail)],
        compiler_params=pltpu.CompilerParams(dimension_semantics=("arbitrary",),
                                             vmem_limit_bytes=VMEM_LIMIT),
        name="ffn_mix" if mix is not None else "ffn",
    )(*args)


def _rope(y, c, s):
    n = y.shape[-1]
    lane = lax.broadcasted_iota(jnp.int32, y.shape, 1)
    first = (lane & (HEAD_DIM - 1)) < HEAD_DIM // 2
    rot = jnp.where(first, pltpu.roll(y, n - HEAD_DIM // 2, 1), pltpu.roll(y, HEAD_DIM // 2, 1))
    return y * c + rot * s


def _proj_kernel(*refs, tm, kb_rows, stacked):
    (x_ref, g_ref, wqkv_ref, wqi_ref, wkw_ref, wconv_ref, wgb_ref, wb2_ref,
     convw_ref, cprev_ref, cos_ref, sin_ref, coskw_ref, sinkw_ref) = refs[:14]
    (kf_ref, vf_ref, kif_ref, kb_ref, kib_ref, q_ref, v_ref, qi_ref, wi_ref,
     gbyb_ref, nconv_ref, ubuf) = refs[17 if stacked else 14:]
    t = pl.program_id(1)
    h = _rms(x_ref[0], g_ref[...]).astype(BF16)
    cosf, sinf = cos_ref[...], sin_ref[...]

    q = _rope(_dot(h, wqkv_ref[:, 0:ATT_W]), cosf, sinf) * Q_SCALE
    k = _rope(_dot(h, wqkv_ref[:, ATT_W:2 * ATT_W]), cosf, sinf)
    v = _dot(h, wqkv_ref[:, 2 * ATT_W:3 * ATT_W])
    qi = _rope(_dot(h, wqi_ref[...]), cosf[:, :IDX_W], sinf[:, :IDX_W])
    kw = _rope(_dot(h, wkw_ref[...]), coskw_ref[...], sinkw_ref[...])

    n_blk = tm // kb_rows
    kb_ref[0] = k.astype(BF16).reshape(n_blk, kb_rows, ATT_W)
    kib_ref[0] = kw[:, :IDX_DIM].astype(BF16).reshape(n_blk, kb_rows, IDX_DIM)

    def tokens_to_lanes(y):
        if tm < LANES:
            y = jnp.concatenate([y, jnp.zeros((LANES - tm, y.shape[1]), y.dtype)], axis=0)
        return y.T

    vt_f32 = tokens_to_lanes(v)
    kwt = tokens_to_lanes(kw)
    if stacked:
        kf_ref[...] = k.T
        vf_ref[...] = vt_f32
        kif_ref[...] = kwt[:IDX_DIM, :]
    else:
        kf_ref[0] = k
        vf_ref[0] = v
        kif_ref[0] = kw[:, :IDX_DIM]

    q_ref[0] = tokens_to_lanes(q).astype(BF16)
    vt = vt_f32.astype(BF16)
    if tm < LANES:
        v_ref[0, 0] = vt
    else:
        for n in range(n_blk):
            v_ref[0, n] = vt[:, n * kb_rows:(n + 1) * kb_rows]
    qi_ref[0] = tokens_to_lanes(qi).astype(BF16)
    wi_ref[0] = kwt[IDX_DIM:IDX_DIM + SUBLANES, :]

    cb = _dot(h, wconv_ref[:, 0:CONV_W])
    cc = _dot(h, wconv_ref[:, CONV_W:2 * CONV_W])
    cx = _dot(h, wconv_ref[:, 2 * CONV_W:3 * CONV_W])
    u = cc * cx

    @pl.when(t == 0)
    def _():
        ubuf[SUBLANES - 2:SUBLANES, :] = cprev_ref[0]

    ubuf[SUBLANES:SUBLANES + tm, :] = u
    um1 = ubuf[SUBLANES - 1:SUBLANES - 1 + tm, :]
    um2 = ubuf[SUBLANES - 2:SUBLANES - 2 + tm, :]
    cw = convw_ref[...]
    y_conv = cw[0:1, :] * um2 + cw[1:2, :] * um1 + cw[2:3, :] * u
    tail = u[tm - 2:tm, :]
    ubuf[SUBLANES - 2:SUBLANES, :] = tail
    nconv_ref[0] = tail

    y_b = _dot((cb * y_conv).astype(BF16), wb2_ref[...])
    g_b = jax.nn.sigmoid(_dot(h, wgb_ref[...]))
    gbyb_ref[0] = (g_b * y_b).astype(BF16)


def _proj_call(x, w, layer, conv_prev, tabs, *, tm, kb_rows, stacked=None):
    b, t, d = x.shape
    nt = t // tm
    cos, sin, coskw, sinkw = tabs
    consts = [w["mix_g"], w["wqkv"], w["wqi"], w["wkw"], w["wconv"], w["wgb"], w["wb2"], w["conv_w"]]
    tile3 = lambda width: pl.BlockSpec((1, tm, width), lambda i, j: (i, j, 0))
    n_blk = tm // kb_rows
    tile4 = lambda width: pl.BlockSpec((1, n_blk, kb_rows, width), lambda i, j: (i, j, 0, 0))
    tab = lambda a: pl.BlockSpec((tm, a.shape[1]), lambda i, j: (j, 0))
    in_specs = ([tile3(d)] + [_layer_spec(c, layer) for c in consts]
                + [pl.BlockSpec((1, CONV_K - 1, CONV_W), lambda i, j: (i, 0, 0))]
                + [tab(cos), tab(sin), tab(coskw), tab(sinkw)])
    args = [x, *consts, conv_prev, cos, sin, coskw, sinkw]
    aliases = {}
    if stacked is None:
        out_shape = [
            jax.ShapeDtypeStruct((b, t, ATT_W), F32),
            jax.ShapeDtypeStruct((b, t, ATT_W), F32),
            jax.ShapeDtypeStruct((b, t, IDX_DIM), F32),
        ]
        out_specs = [tile3(ATT_W), tile3(ATT_W), tile3(IDX_DIM)]
    else:
        out_shape = [jax.ShapeDtypeStruct(a.shape, a.dtype) for a in stacked]
        out_specs = [pl.BlockSpec((None, None, a.shape[2], tm), lambda i, j: (layer, i, 0, j)) for a in stacked]
        aliases = {len(args) + n: n for n in range(len(stacked))}
        in_specs += [pl.BlockSpec(memory_space=pl.ANY) for _ in stacked]
        args += list(stacked)
    out_shape += [
        jax.ShapeDtypeStruct((b, t // kb_rows, kb_rows, ATT_W), BF16),
        jax.ShapeDtypeStruct((b, t // kb_rows, kb_rows, IDX_DIM), BF16),
    ]
    out_specs += [tile4(ATT_W), tile4(IDX_DIM)]
    tl = max(tm, LANES)
    vt_blocks, vt_cols = ((nt, tl), (1, tl)) if tm < LANES else ((t // kb_rows, kb_rows), (n_blk, kb_rows))
    lanes3 = lambda rows: pl.BlockSpec((1, rows, tl), lambda i, j: (i, 0, j))
    out_shape += [
        jax.ShapeDtypeStruct((b, ATT_W, nt * tl), BF16),
        jax.ShapeDtypeStruct((b, vt_blocks[0], ATT_W, vt_blocks[1]), BF16),
        jax.ShapeDtypeStruct((b, IDX_W, nt * tl), BF16),
        jax.ShapeDtypeStruct((b, SUBLANES, nt * tl), F32),
    ]
    out_specs += [lanes3(ATT_W),
                  pl.BlockSpec((1, vt_cols[0], ATT_W, vt_cols[1]), lambda i, j: (i, j, 0, 0)),
                  lanes3(IDX_W), lanes3(SUBLANES)]
    out_shape += [
        jax.ShapeDtypeStruct((b, t, d), BF16),
        jax.ShapeDtypeStruct((b, CONV_K - 1, CONV_W), F32),
    ]
    out_specs += [tile3(d), pl.BlockSpec((1, CONV_K - 1, CONV_W), lambda i, j: (i, 0, 0))]
    kern = functools.partial(_proj_kernel, tm=tm, kb_rows=kb_rows, stacked=stacked is not None)
    return pl.pallas_call(
        kern,
        out_shape=out_shape,
        grid=(b, nt),
        in_specs=in_specs,
        out_specs=out_specs,
        scratch_shapes=[pltpu.VMEM((tm + SUBLANES, CONV_W), F32)],
        input_output_aliases=aliases,
        compiler_params=pltpu.CompilerParams(dimension_semantics=("arbitrary", "arbitrary"),
                                             vmem_limit_bytes=VMEM_LIMIT),
        name="proj",
    )(*args)


def _slab_reduce(x, op):
    kb, qw = x.shape
    return op(x.reshape(kb // SUBLANES, SUBLANES, qw), axis=0)


def _attend_kernel(*refs, kb_rows, qw, n_kb_static, q_pos0, n_valid, top_k, assemble):
    j = pl.program_id(1)
    n_kb = (j + 1) if n_kb_static is None else n_kb_static
    kf = float(top_k)

    if assemble:
        (qi_ref, wi_ref, kic_ref, kin_ref, kc_ref, kn_ref, q_ref, vtc_ref, vtn_ref,
         o_ref, s_ref, lg_ref, ot_ref, ki_s, k_s, v_s) = refs
        past, new = kc_ref.shape[1], kn_ref.shape[0]
        for dst, cached, fresh in ((ki_s, kic_ref[...], kin_ref), (k_s, kc_ref[...].T.astype(BF16), kn_ref)):
            dst[0, 0:past, :] = cached
            dst[0, past:past + new, :] = fresh[...]
            dst[0, past + new:, :] = jnp.zeros((kb_rows - past - new, dst.shape[2]), BF16)
        v_s[0, :, 0:past] = vtc_ref[...].astype(BF16)
        v_s[0, :, past:] = vtn_ref[...]
        ki_blk = lambda kb: ki_s[kb]
        k_blk = lambda kb, lanes: k_s[kb, :, lanes]
        v_blk = lambda kb, rows: v_s[kb, rows, :]
    else:
        qi_ref, wi_ref, ki_ref, k_ref, q_ref, v_ref, o_ref, s_ref, lg_ref, ot_ref = refs
        ki_blk = lambda kb: ki_ref[0, kb]
        k_blk = lambda kb, lanes: k_ref[0, kb, :, lanes]
        v_blk = lambda kb, rows: v_ref[0, kb, rows, :]

    def over_blocks(body, init, n=None):
        n = n_kb if n is None else n
        pairs = lax.fori_loop(0, n // 2, lambda i, c: body([2 * i, 2 * i + 1], c), init)
        if isinstance(n, int):
            return body([n - 1], pairs) if n % 2 else pairs
        return lax.cond(n % 2 == 1, lambda c: body([n - 1], c), lambda c: c, pairs)

    last = n_kb - 1
    row = lax.broadcasted_iota(jnp.int32, (kb_rows, qw), 0) + last * kb_rows
    qpos = lax.broadcasted_iota(jnp.int32, (kb_rows, qw), 1) + (q_pos0 + j * qw)
    chunk_shift = CHUNK.bit_length() - 1
    adm_last = ((row >> chunk_shift) <= (qpos >> chunk_shift)) & (row < n_valid)

    def score_block(kb):
        ki = ki_blk(kb)
        s = jnp.zeros((kb_rows, qw), F32)
        for hh in range(IDX_HEADS):
            d = _dot(ki, qi_ref[0, hh * IDX_DIM:(hh + 1) * IDX_DIM, :])
            s = s + jnp.maximum(d, 0.0) * wi_ref[0, hh:hh + 1, :]
        return s * IDX_SCALE

    def score_body(kbs, carry):
        mx, mn = carry
        for kb in kbs:
            s = score_block(kb)
            s_ref[kb] = s
            mx = jnp.maximum(mx, _slab_reduce(s, jnp.max))
            mn = jnp.minimum(mn, _slab_reduce(s, jnp.min))
        return mx, mn

    mx8 = jnp.full((SUBLANES, qw), -BIG, F32)
    mn8 = jnp.full((SUBLANES, qw), BIG, F32)
    mx8, mn8 = over_blocks(score_body, (mx8, mn8), n=last)
    s_last = score_block(last)
    s_ref[last] = jnp.where(adm_last, s_last, NEG)
    mx8 = jnp.maximum(mx8, _slab_reduce(jnp.where(adm_last, s_last, -BIG), jnp.max))
    mn8 = jnp.minimum(mn8, _slab_reduce(jnp.where(adm_last, s_last, BIG), jnp.min))
    smax = jnp.max(mx8, axis=0, keepdims=True)
    smin = jnp.min(mn8, axis=0, keepdims=True)
    n_adm = (jnp.sum(_slab_reduce(jnp.where(adm_last, 1.0, 0.0), jnp.sum), axis=0, keepdims=True)
             + jnp.float32(1.0) * (last * kb_rows))

    def count(pred_fn):
        def body(kbs, acc):
            for kb in kbs:
                acc = acc + _slab_reduce(jnp.where(pred_fn(s_ref[kb]), 1.0, 0.0), jnp.sum)
            return acc
        return jnp.sum(over_blocks(body, jnp.zeros((SUBLANES, qw), F32)), axis=0, keepdims=True)

    def bisect(_, carry):
        lo, hi, c_hi = carry
        mid = 0.5 * lo + 0.5 * hi
        cnt = count(lambda s: s >= mid)
        ge = cnt >= kf
        return jnp.where(ge, mid, lo), jnp.where(ge, hi, mid), jnp.where(ge, c_hi, cnt)

    hi0 = smax + (jnp.abs(smax) + 1.0) * (2.0 ** -10)
    n_bisect = jnp.where(q_pos0 + (j + 1) * qw <= top_k, 0, N_BISECT)
    _, hi, c_hi = lax.fori_loop(0, n_bisect, bisect, (smin, hi0, jnp.zeros((1, qw), F32)))

    def snap(carry):
        hi, c_hi, done, tau, n_gt = carry

        def vbody(kbs, acc):
            for kb in kbs:
                s = s_ref[kb]
                acc = jnp.maximum(acc, _slab_reduce(jnp.where(s < hi, s, -BIG), jnp.max))
            return acc
        cand = jnp.max(over_blocks(vbody, jnp.full((SUBLANES, qw), -BIG, F32)), axis=0, keepdims=True)
        c_ge = count(lambda s: s >= cand)
        ok = c_ge >= kf
        fresh = ok & (done < 0.5)
        return (jnp.where(ok, hi, cand), jnp.where(ok, c_hi, c_ge), jnp.where(ok, 1.0, done),
                jnp.where(fresh, cand, tau), jnp.where(fresh, c_hi, n_gt))

    few = n_adm <= kf
    carry0 = (hi, c_hi, jnp.where(few, 1.0, 0.0), jnp.full((1, qw), 0.5 * NEG, F32), jnp.zeros((1, qw), F32))
    _, _, _, tau, n_gt = lax.while_loop(lambda c: jnp.min(c[2]) < 0.5, snap, carry0)
    need = kf - n_gt

    r_i = lax.broadcasted_iota(jnp.int32, (LANES, LANES), 0)
    c_i = lax.broadcasted_iota(jnp.int32, (LANES, LANES), 1)
    tri = jnp.where(c_i <= r_i, 1.0, 0.0).astype(BF16)

    def select_body(kbs, off):
        for kb in kbs:
            for r0 in range(0, kb_rows, LANES):
                s = s_ref[kb, r0:r0 + LANES, :]
                eq = s == tau
                rank = _dot(tri, jnp.where(eq, 1.0, 0.0).astype(BF16))
                y = jnp.where(eq, rank + off, jnp.where(s > tau, 0.0, BIG))
                s_ref[kb, r0:r0 + LANES, :] = jnp.where(y <= need, 0.0, NEG)
                off = off + rank[LANES - 1:LANES, :]
        return off

    over_blocks(select_body, jnp.zeros((1, qw), F32))

    pair_w = 2 * HEAD_DIM
    pr = lax.broadcasted_iota(jnp.int32, (pair_w, qw), 0)
    q_pairs = []
    for pair in range(N_HEADS // 2):
        qp = q_ref[0, pair * pair_w:(pair + 1) * pair_w, :]
        zero = jnp.zeros_like(qp)
        q_pairs.append(jnp.concatenate([jnp.where(pr < HEAD_DIM, qp, zero),
                                        jnp.where(pr >= HEAD_DIM, qp, zero)], axis=1))

    ot_ref[...] = jnp.zeros(ot_ref.shape, F32)
    head_rows = [slice(hd * HEAD_DIM, (hd + 1) * HEAD_DIM) for hd in range(N_HEADS)]

    def logit_body(kbs, m8s):
        m8s = list(m8s)
        for kb in kbs:
            bias = s_ref[kb]
            for pair in range(N_HEADS // 2):
                lg2 = _dot(k_blk(kb, slice(pair * pair_w, (pair + 1) * pair_w)), q_pairs[pair])
                for half in range(2):
                    hd = 2 * pair + half
                    lg = lg2[:, half * qw:(half + 1) * qw] + bias
                    lg_ref[hd, kb] = lg
                    m8s[hd] = jnp.maximum(m8s[hd], _slab_reduce(lg, jnp.max))
        return tuple(m8s)

    m8s = over_blocks(logit_body, tuple(jnp.full((SUBLANES, qw), -BIG, F32) for _ in range(N_HEADS)))
    ms = [jnp.max(m8, axis=0, keepdims=True) for m8 in m8s]

    def pv_body(kbs, l8s):
        l8s = list(l8s)
        for hd in range(N_HEADS):
            rows = head_rows[hd]
            acc = ot_ref[rows, :]
            for kb in kbs:
                e = jnp.exp2(lg_ref[hd, kb] - ms[hd])
                acc = acc + _dot(v_blk(kb, rows), e.astype(BF16))
                l8s[hd] = l8s[hd] + _slab_reduce(e, jnp.sum)
            ot_ref[rows, :] = acc
        return tuple(l8s)

    l8s = over_blocks(pv_body, tuple(jnp.zeros((SUBLANES, qw), F32) for _ in range(N_HEADS)))
    for hd in range(N_HEADS):
        rows = head_rows[hd]
        ot_ref[rows, :] = ot_ref[rows, :] / jnp.sum(l8s[hd], axis=0, keepdims=True)

    o_ref[0] = ot_ref[...].T.astype(BF16)


def _attend_call(qi_t, wi_t, q_t, keys, *, layer=None, qw, kb_rows, n_kb_max, n_kb_static,
                 q_pos0, n_valid, top_k):
    b, _, t_q = q_t.shape
    n_q = t_q // qw
    assemble = layer is not None
    assert qw % CHUNK == 0 and q_pos0 % CHUNK == 0
    qspec = lambda rows: pl.BlockSpec((1, rows, qw), lambda i, j: (i, 0, j))
    scratch = [pltpu.VMEM((n_kb_max, kb_rows, qw), F32), pltpu.VMEM((N_HEADS, n_kb_max, kb_rows, qw), F32),
               pltpu.VMEM((ATT_W, qw), F32)]
    if assemble:
        kic, kc, vtc, kin, kn, vtn = keys
        cache = lambda a: pl.BlockSpec((None, None) + a.shape[2:], lambda i, j: (layer, i, 0, 0))
        fresh = lambda a: pl.BlockSpec((None,) + a.shape[1:], lambda i, j: (i, 0, 0))
        args = [qi_t, wi_t, kic, kin, kc, kn, q_t, vtc, vtn]
        in_specs = [qspec(IDX_W), qspec(SUBLANES), cache(kic), fresh(kin), cache(kc), fresh(kn),
                    qspec(ATT_W), cache(vtc), fresh(vtn)]
        scratch += [pltpu.VMEM((1, kb_rows, IDX_DIM), BF16), pltpu.VMEM((1, kb_rows, ATT_W), BF16),
                    pltpu.VMEM((1, ATT_W, kb_rows), BF16)]
    else:
        ki_b, k_b, v_tb = keys
        batch_blk = lambda a: pl.BlockSpec((1,) + a.shape[1:], lambda i, j: (i, 0, 0, 0))
        args = [qi_t, wi_t, ki_b, k_b, q_t, v_tb]
        in_specs = [qspec(IDX_W), qspec(SUBLANES), batch_blk(ki_b), batch_blk(k_b), qspec(ATT_W),
                    batch_blk(v_tb)]
    kern = functools.partial(_attend_kernel, kb_rows=kb_rows, qw=qw, n_kb_static=n_kb_static,
                             q_pos0=q_pos0, n_valid=n_valid, top_k=top_k, assemble=assemble)
    return pl.pallas_call(
        kern,
        out_shape=jax.ShapeDtypeStruct((b, t_q, ATT_W), BF16),
        grid=(b, n_q),
        in_specs=in_specs,
        out_specs=pl.BlockSpec((1, qw, ATT_W), lambda i, j: (i, j, 0)),
        scratch_shapes=scratch,
        compiler_params=pltpu.CompilerParams(dimension_semantics=("arbitrary", "arbitrary"),
                                             vmem_limit_bytes=VMEM_LIMIT),
        name="attend_cached" if assemble else "attend",
    )(*args)


def _rope_tables(pos):
    half = HEAD_DIM // 2
    freqs = ROPE_THETA ** (-jnp.arange(half, dtype=F32) / half)
    ang = pos.astype(F32)[:, None] * freqs[None, :]
    cos, sin = jnp.cos(ang), jnp.sin(ang)
    cos_h = jnp.concatenate([cos, cos], axis=-1)
    sin_h = jnp.concatenate([-sin, sin], axis=-1)
    n = pos.shape[0]
    pad_c = jnp.ones((n, LANES - IDX_DIM), F32)
    pad_s = jnp.zeros((n, LANES - IDX_DIM), F32)
    return (jnp.tile(cos_h, (1, N_HEADS)), jnp.tile(sin_h, (1, N_HEADS)),
            jnp.concatenate([cos_h, pad_c], axis=-1), jnp.concatenate([sin_h, pad_s], axis=-1))


def _prep_weights(ffn1_norm, ffn1_up, ffn1_down, mix_norm, w_in, conv_w, w_branch, w_out,
                  ffn2_norm, ffn2_up, ffn2_down):
    d = w_in.shape[1]
    c = lambda a: a.astype(BF16)
    gain = lambda a: a[:, None, :]
    sizes = (3 * ATT_W, IDX_W, IDX_DIM + IDX_HEADS, 3 * CONV_W, d, d)
    offs = [0]
    for s in sizes:
        offs.append(offs[-1] + s)
    wqkv, wqi, wkw, wconv, wga, wgb = (w_in[:, :, offs[n]:offs[n + 1]] for n in range(len(sizes)))
    wkw = jnp.pad(wkw, ((0, 0), (0, 0), (0, LANES - IDX_DIM - IDX_HEADS)))
    return dict(
        ffn1_g=gain(ffn1_norm), ffn1_up=c(ffn1_up), ffn1_dn=c(ffn1_down),
        mix_g=gain(mix_norm), wqkv=c(wqkv), wqi=c(wqi), wkw=c(wkw), wconv=c(wconv),
        wga=c(wga), wgb=c(wgb), conv_w=conv_w,
        wb1=c(w_branch[:, :ATT_W]), wb2=c(w_branch[:, ATT_W:]), wout=c(w_out),
        ffn2_g=gain(ffn2_norm), ffn2_up=c(ffn2_up), ffn2_dn=c(ffn2_down),
    )


def _pick_tile(n, pref):
    t = min(pref, n)
    while n % t:
        t //= 2
    return t


def kernel(x_prompt, x_sample, cache_k, cache_v, cache_kidx, state_conv, ffn1_norm, ffn1_up, ffn1_down, mix_norm, w_in, conv_w, w_branch, w_out, ffn2_norm, ffn2_up, ffn2_down, final_norm):
    depth = w_in.shape[0]
    bp, tp, d = x_prompt.shape
    bs, ts, _ = x_sample.shape
    past = cache_k.shape[2]
    assert past % LANES == 0 and ts % (2 * SUBLANES) == 0 and ts <= LANES
    qw_p = 2 * LANES
    qw_s = LANES
    l_s = past + ts
    kb_s = -(-l_s // LANES) * LANES
    topk_p = min(TOPK_MAX, tp // 4)
    topk_s = min(TOPK_MAX, l_s // 4)

    w = _prep_weights(ffn1_norm, ffn1_up, ffn1_down, mix_norm, w_in, conv_w, w_branch, w_out,
                      ffn2_norm, ffn2_up, ffn2_down)
    tabs_p = _rope_tables(jnp.arange(tp))
    tabs_s = _rope_tables(past + jnp.arange(ts))
    fin = final_norm[None]
    zero_conv = jnp.zeros((bp, CONV_K - 1, CONV_W), F32)
    tm_p = _pick_tile(bp * tp, 512)
    tm_proj = 2 * qw_p if tp % (2 * qw_p) == 0 else qw_p

    kc_all = jnp.swapaxes(cache_k.reshape(depth, bs, past, ATT_W), 2, 3)
    vtc_all = jnp.swapaxes(cache_v.reshape(depth, bs, past, ATT_W), 2, 3)
    kic_all = cache_kidx.astype(BF16)

    xp = x_prompt.reshape(bp * tp, d)
    xs = x_sample.reshape(bs * ts, d)
    kvi = tuple(lax.empty((depth, bp, width, tp), F32) for width in (ATT_W, ATT_W, IDX_DIM))
    outs = [[] for _ in range(5)]
    for i in range(depth):
        last = i == depth - 1

        xp, xs = _ffn_call(xp, xs, w, i, "ffn1", tm=tm_p)

        (*kvi, kb, kib, qt, vtb, qit, wit, gbyb_p, nconv) = _proj_call(
            xp.reshape(bp, tp, d), w, i, zero_conv, tabs_p, tm=tm_proj, kb_rows=qw_p, stacked=kvi)
        oatt_p = _attend_call(qit, wit, qt, (kib, kb, vtb), qw=qw_p, kb_rows=qw_p, n_kb_max=tp // qw_p,
                              n_kb_static=None, q_pos0=0, n_valid=tp, top_k=topk_p)
        outs[0].append(nconv)

        (kf, vf, kif, kb, kib, qt, vtb, qit, wit, gbyb_s, nconv) = _proj_call(
            xs.reshape(bs, ts, d), w, i, state_conv[i], tabs_s, tm=ts, kb_rows=ts)
        oatt_s = _attend_call(qit, wit, qt, (kic_all, kc_all, vtc_all, kib[:, 0], kb[:, 0], vtb[:, 0]),
                              layer=i, qw=qw_s, kb_rows=kb_s, n_kb_max=1, n_kb_static=1,
                              q_pos0=past, n_valid=l_s, top_k=topk_s)[:, :ts]
        for lst, a in zip(outs[1:], (kf, vf, kif, nconv)):
            lst.append(a)

        xp, xs = _ffn_call(xp, xs, w, i, "ffn2",
                           mix=(oatt_p.reshape(bp * tp, ATT_W), gbyb_p.reshape(bp * tp, d)),
                           mix_tail=(oatt_s.reshape(bs * ts, ATT_W), gbyb_s.reshape(bs * ts, d)),
                           final_g=fin if last else None, tm=tm_p)

    heads = lambda a: a.reshape(a.shape[:-1] + (N_HEADS, HEAD_DIM))
    heads_t = lambda a: jnp.transpose(a.reshape(depth, bp, N_HEADS, HEAD_DIM, tp), (0, 1, 4, 2, 3))
    st = [jnp.stack(l) for l in outs]
    return (xp.reshape(bp, tp, d), xs.reshape(bs, ts, d),
            heads_t(kvi[0]), heads_t(kvi[1]), jnp.swapaxes(kvi[2], 2, 3), st[0],
            heads(st[1]), heads(st[2]), st[3], st[4])
```
